```python
import math
import jax, jax.numpy as jnp
from jax import lax
import numpy as np

D_MODEL = 2048
BATCH = 1
SEQ = 8192
DEPTH = 1
DEC_BATCH = 32
DEC_SEQ = 4
PAST_LEN = 16384
PAGE_SIZE = 128

N_META = 16
HEAD_DIM = 128
DIFF_HEADS = 8
DIFF_KV_HEADS = 2
DIFF_REP = DIFF_HEADS // DIFF_KV_HEADS
DIFF_V_DIM = 2 * HEAD_DIM
SB_HEADS = 16
SB_KV_HEADS = 4
SB_REP = SB_HEADS // SB_KV_HEADS
DIFF_OUT = DIFF_HEADS * DIFF_V_DIM
SB_OUT = SB_HEADS * HEAD_DIM
W_QD = DIFF_HEADS * 2 * HEAD_DIM
W_KD = DIFF_KV_HEADS * 2 * HEAD_DIM
W_VD = DIFF_KV_HEADS * DIFF_V_DIM
W_QS = SB_HEADS * HEAD_DIM
W_KS = SB_KV_HEADS * HEAD_DIM
W_VS = SB_KV_HEADS * HEAD_DIM
W_GATES = 2 * D_MODEL
IN_COLS = W_QD + W_KD + W_VD + W_QS + W_KS + W_VS + W_GATES
Q_BLOCK = 128
ROPE_THETA = 10000.0
N_GROUPS = 4
EXPERTS_PER_GROUP = 8
N_EXPERTS = N_GROUPS * EXPERTS_PER_GROUP
TOP_K = 2
D_EXPERT = 512
RMS_EPS = 1e-6
NEG_INF = -1e30

kernel_name = 'hybrid_diffattn_stickbreak_hmoe_step'


def rms_norm(x, g):
    xf = x.astype(jnp.float32)
    y = xf * lax.rsqrt(jnp.mean(xf * xf, axis=-1, keepdims=True) + RMS_EPS)
    return (y * g.astype(jnp.float32)).astype(x.dtype)


def rope(x, pos):
    half = HEAD_DIM // 2
    inv = ROPE_THETA ** (-jnp.arange(half, dtype=jnp.float32) * 2.0 / HEAD_DIM)
    ang = pos.astype(jnp.float32)[:, None] * inv[None, :]
    ang = ang.reshape((1, pos.shape[0]) + (1,) * (x.ndim - 3) + (half,))
    cos, sin = jnp.cos(ang), jnp.sin(ang)
    xf = x.astype(jnp.float32)
    x1, x2 = xf[..., :half], xf[..., half:]
    return jnp.concatenate([x1 * cos - x2 * sin, x2 * cos + x1 * sin], axis=-1).astype(x.dtype)


def project_tokens(x, pos, norm_g, w_in):
    b, t, _ = x.shape
    h = rms_norm(x, norm_g)
    z = h @ w_in
    bounds = [int(i) for i in np.cumsum([W_QD, W_KD, W_VD, W_QS, W_KS, W_VS])]
    qd, kd, vd, qs, ks, vs, gates = jnp.split(z, bounds, axis=-1)
    qd = rope(qd.reshape(b, t, DIFF_KV_HEADS, DIFF_REP, 2, HEAD_DIM), pos)
    kd = rope(kd.reshape(b, t, DIFF_KV_HEADS, 2, HEAD_DIM), pos)
    vd = vd.reshape(b, t, DIFF_KV_HEADS, DIFF_V_DIM)
    qs = qs.reshape(b, t, SB_KV_HEADS, SB_REP, HEAD_DIM)
    ks = ks.reshape(b, t, SB_KV_HEADS, HEAD_DIM)
    vs = vs.reshape(b, t, SB_KV_HEADS, HEAD_DIM)
    return qd, kd, vd, qs, ks, vs, gates


def attend_block(qd, qs, pos_q, segments, lam, lam_init, subln_g):
    f32 = jnp.float32
    b, tq = qd.shape[:2]
    scale = HEAD_DIM ** -0.5
    pos_k = jnp.concatenate([seg[4] for seg in segments])
    splits = []
    acc = 0
    for seg in segments[:-1]:
        acc += seg[4].shape[0]
        splits.append(acc)

    sd = jnp.concatenate([jnp.einsum('bqgrmd,bkgmd->bgrmqk', qd, seg[0], preferred_element_type=f32)
                          for seg in segments], axis=-1) * scale
    causal = pos_k[None, :] <= pos_q[:, None]
    pd = jax.nn.softmax(jnp.where(causal, sd, NEG_INF), axis=-1)
    wd = pd[:, :, :, 0] - lam * pd[:, :, :, 1]
    od = sum(jnp.einsum('bgrqk,bkgd->bqgrd', w, seg[1].astype(f32))
             for w, seg in zip(jnp.split(wd, splits, axis=-1), segments))
    od = od * lax.rsqrt(jnp.mean(od * od, axis=-1, keepdims=True) + RMS_EPS)
    od = od * subln_g.astype(f32) * (1.0 - lam_init)
    od = od.reshape(b, tq, DIFF_OUT)

    zs = jnp.concatenate([jnp.einsum('bqgrd,bkgd->bgrqk', qs, seg[2], preferred_element_type=f32)
                          for seg in segments], axis=-1) * scale
    strict = pos_k[None, :] < pos_q[:, None]
    log_keep = jnp.where(strict, jax.nn.log_sigmoid(-zs), 0.0)
    suffix = lax.cumsum(log_keep, axis=zs.ndim - 1, reverse=True) - log_keep
    a = jnp.where(strict, jnp.exp(jax.nn.log_sigmoid(zs) + suffix), 0.0)
    os_ = sum(jnp.einsum('bgrqk,bkgd->bqgrd', w, seg[3].astype(f32))
              for w, seg in zip(jnp.split(a, splits, axis=-1), segments))
    return od, os_.reshape(b, tq, SB_OUT)


def prompt_attention(qd, qs, pos, segment, lam, lam_init, subln_g):
    b = qd.shape[0]
    n_blocks = (pos.shape[0] - N_META) // Q_BLOCK
    od_m, os_m = attend_block(qd[:, :N_META], qs[:, :N_META], pos[:N_META], (segment,), lam, lam_init, subln_g)

    def to_blocks(a):
        a = a[:, N_META:]
        return jnp.moveaxis(a.reshape((b, n_blocks, Q_BLOCK) + a.shape[2:]), 1, 0)

    def one_block(args):
        q_d, q_s, p = args
        return attend_block(q_d, q_s, p, (segment,), lam, lam_init, subln_g)

    od_b, os_b = lax.map(one_block, (to_blocks(qd), to_blocks(qs), pos[N_META:].reshape(n_blocks, Q_BLOCK)))

    def from_blocks(a):
        return jnp.moveaxis(a, 0, 1).reshape(b, n_blocks * Q_BLOCK, a.shape[-1])

    od = jnp.concatenate([od_m, from_blocks(od_b)], axis=1)
    os_ = jnp.concatenate([os_m, from_blocks(os_b)], axis=1)
    return od, os_


def gather_pages(cache, layer, page_table):
    rows = cache[layer, page_table]
    return rows.reshape((page_table.shape[0], page_table.shape[1] * cache.shape[2]) + cache.shape[3:])


def hier_moe(h, w_rg, b_rg, w_re, b_re, w_g, w_u, w_d):
    f32 = jnp.float32
    b, t, d = h.shape
    n = h.reshape(-1, d)
    g_logits = (n @ w_rg).astype(f32) + b_rg.astype(f32)
    grp = jnp.argmax(g_logits, axis=-1)
    p_grp = jnp.take_along_axis(jax.nn.softmax(g_logits, axis=-1), grp[:, None], axis=-1)
    e_logits = ((n @ w_re).astype(f32) + b_re.astype(f32)).reshape(-1, N_GROUPS, EXPERTS_PER_GROUP)
    e_sel = jnp.take_along_axis(e_logits, grp[:, None, None], axis=1)[:, 0]
    top_v, top_i = lax.top_k(e_sel, TOP_K)
    gate = p_grp * jax.nn.softmax(top_v, axis=-1)
    eid = grp[:, None] * EXPERTS_PER_GROUP + top_i
    combine = jnp.sum(jax.nn.one_hot(eid, N_EXPERTS, dtype=f32) * gate[..., None], axis=1)
    hid = jax.nn.silu(jnp.einsum('nd,edf->nef', n, w_g)) * jnp.einsum('nd,edf->nef', n, w_u)
    out = jnp.einsum('nef,efd->nd', hid * combine[..., None], w_d)
    return out.reshape(b, t, d).astype(h.dtype)


def residual_block(x, od, os_, gates, w_branch_diff, w_branch_sb, w_out, norm_ffn_g,
                   w_router_group, b_router_group, w_router_expert, b_router_expert,
                   w_expert_gate, w_expert_up, w_expert_down):
    g = jax.nn.sigmoid(gates.astype(jnp.float32))
    g_d, g_s = g[..., :D_MODEL], g[..., D_MODEL:]
    merged = (g_d * (od.astype(x.dtype) @ w_branch_diff).astype(jnp.float32)
              + g_s * (os_.astype(x.dtype) @ w_branch_sb).astype(jnp.float32))
    x = x + merged.astype(x.dtype) @ w_out
    return x + hier_moe(rms_norm(x, norm_ffn_g), w_router_group, b_router_group, w_router_expert,
                        b_router_expert, w_expert_gate, w_expert_up, w_expert_down)


def setup_inputs(seed: int = 0) -> dict:
    key = jax.random.key(seed)
    ks = jax.random.split(key, 28)
    f32 = jnp.float32
    n_pages = PAST_LEN // PAGE_SIZE
    n_phys = (DEC_BATCH * n_pages * 5) // 4

    def nrm(k, shape, scale=1.0):
        return jax.random.normal(k, shape, f32) * scale

    def gain(k, shape):
        return 1.0 + nrm(k, shape, 0.01)

    page_table = jax.random.permutation(ks[6], n_phys)[: DEC_BATCH * n_pages]
    page_table = page_table.reshape(DEC_BATCH, n_pages).astype(jnp.int32)
    return {
        'x_prompt': nrm(ks[0], (BATCH, SEQ, D_MODEL)),
        'x_sample': nrm(ks[1], (DEC_BATCH, DEC_SEQ, D_MODEL)),
        'cache_k_diff': nrm(ks[2], (DEPTH, n_phys, PAGE_SIZE, DIFF_KV_HEADS, 2, HEAD_DIM)),
        'cache_v_diff': nrm(ks[3], (DEPTH, n_phys, PAGE_SIZE, DIFF_KV_HEADS, DIFF_V_DIM)),
        'cache_k_sb': nrm(ks[4], (DEPTH, n_phys, PAGE_SIZE, SB_KV_HEADS, HEAD_DIM)),
        'cache_v_sb': nrm(ks[5], (DEPTH, n_phys, PAGE_SIZE, SB_KV_HEADS, HEAD_DIM)),
        'page_table': page_table,
        'meta_tokens': nrm(ks[7], (N_META, D_MODEL)),
        'norm_mix_g': gain(ks[8], (DEPTH, D_MODEL)),
        'w_in': nrm(ks[9], (DEPTH, D_MODEL, IN_COLS), D_MODEL ** -0.5),
        'lambda_q1': nrm(ks[10], (DEPTH, HEAD_DIM), 0.1),
        'lambda_k1': nrm(ks[11], (DEPTH, HEAD_DIM), 0.1),
        'lambda_q2': nrm(ks[12], (DEPTH, HEAD_DIM), 0.1),
        'lambda_k2': nrm(ks[13], (DEPTH, HEAD_DIM), 0.1),
        'subln_g': gain(ks[14], (DEPTH, DIFF_V_DIM)),
        'w_branch_diff': nrm(ks[15], (DEPTH, DIFF_OUT, D_MODEL), DIFF_OUT ** -0.5),
        'w_branch_sb': nrm(ks[16], (DEPTH, SB_OUT, D_MODEL), SB_OUT ** -0.5),
        'w_out': nrm(ks[17], (DEPTH, D_MODEL, D_MODEL), D_MODEL ** -0.5),
        'norm_ffn_g': gain(ks[18], (DEPTH, D_MODEL)),
        'w_router_group': nrm(ks[19], (DEPTH, D_MODEL, N_GROUPS), D_MODEL ** -0.5),
        'b_router_group': nrm(ks[20], (DEPTH, N_GROUPS), 0.01),
        'w_router_expert': nrm(ks[21], (DEPTH, D_MODEL, N_EXPERTS), D_MODEL ** -0.5),
        'b_router_expert': nrm(ks[22], (DEPTH, N_EXPERTS), 0.01),
        'w_expert_gate': nrm(ks[23], (DEPTH, N_EXPERTS, D_MODEL, D_EXPERT), D_MODEL ** -0.5),
        'w_expert_up': nrm(ks[24], (DEPTH, N_EXPERTS, D_MODEL, D_EXPERT), D_MODEL ** -0.5),
        'w_expert_down': nrm(ks[25], (DEPTH, N_EXPERTS, D_EXPERT, D_MODEL), D_EXPERT ** -0.5),
        'norm_final_g': gain(ks[26], (D_MODEL,)),
    }


def reference(x_prompt, x_sample, cache_k_diff, cache_v_diff, cache_k_sb, cache_v_sb, page_table,
              meta_tokens, norm_mix_g, w_in, lambda_q1, lambda_k1, lambda_q2, lambda_k2, subln_g,
              w_branch_diff, w_branch_sb, w_out, norm_ffn_g, w_router_group, b_router_group,
              w_router_expert, b_router_expert, w_expert_gate, w_expert_up, w_expert_down,
              norm_final_g):
    f32 = jnp.float32
    b = x_prompt.shape[0]
    meta = jnp.broadcast_to(meta_tokens.astype(x_prompt.dtype)[None], (b, N_META, D_MODEL))
    xp = jnp.concatenate([meta, x_prompt], axis=1)
    xs = x_sample
    pos_p = jnp.arange(xp.shape[1], dtype=jnp.int32)
    past_len = page_table.shape[1] * cache_k_diff.shape[2]
    pos_past = jnp.arange(past_len, dtype=jnp.int32)
    pos_s = past_len + jnp.arange(xs.shape[1], dtype=jnp.int32)

    kdp, vdp, ksp, vsp, kds, vds, kss, vss = [], [], [], [], [], [], [], []
    for l in range(DEPTH):
        lam_init = 0.8 - 0.6 * math.exp(-0.3 * l)
        lam = (jnp.exp(jnp.sum(lambda_q1[l].astype(f32) * lambda_k1[l].astype(f32)))
               - jnp.exp(jnp.sum(lambda_q2[l].astype(f32) * lambda_k2[l].astype(f32))) + lam_init)
        ffn = (w_branch_diff[l], w_branch_sb[l], w_out[l], norm_ffn_g[l], w_router_group[l],
               b_router_group[l], w_router_expert[l], b_router_expert[l], w_expert_gate[l],
               w_expert_up[l], w_expert_down[l])

        qd, kd, vd, qs, ks, vs, gates = project_tokens(xp, pos_p, norm_mix_g[l], w_in[l])
        od, os_ = prompt_attention(qd, qs, pos_p, (kd, vd, ks, vs, pos_p), lam, lam_init, subln_g[l])
        xp = residual_block(xp, od, os_, gates, *ffn)
        kdp.append(kd)
        vdp.append(vd)
        ksp.append(ks)
        vsp.append(vs)

        qd, kd, vd, qs, ks, vs, gates = project_tokens(xs, pos_s, norm_mix_g[l], w_in[l])
        past = (gather_pages(cache_k_diff, l, page_table), gather_pages(cache_v_diff, l, page_table),
                gather_pages(cache_k_sb, l, page_table), gather_pages(cache_v_sb, l, page_table), pos_past)
        od, os_ = attend_block(qd, qs, pos_s, (past, (kd, vd, ks, vs, pos_s)), lam, lam_init, subln_g[l])
        xs = residual_block(xs, od, os_, gates, *ffn)
        kds.append(kd)
        vds.append(vd)
        kss.append(ks)
        vss.append(vs)

    y_prompt = rms_norm(xp, norm_final_g)[:, N_META:]
    y_sample = rms_norm(xs, norm_final_g)
    return (y_prompt, y_sample, jnp.stack(kdp), jnp.stack(vdp), jnp.stack(ksp), jnp.stack(vsp),
            jnp.stack(kds), jnp.stack(vds), jnp.stack(kss), jnp.stack(vss))
```

```python
import functools
import math

import jax
import jax.numpy as jnp
import numpy as np
from jax import lax
from jax.experimental import pallas as pl
from jax.experimental.pallas import tpu as pltpu

F32 = jnp.float32
BF16 = jnp.bfloat16

D_MODEL = 2048
N_META = 16
HEAD_DIM = 128
DIFF_KV_HEADS = 2
DIFF_REP = 4
DIFF_V_DIM = 2 * HEAD_DIM
SB_KV_HEADS = 4
SB_REP = 4
ROPE_THETA = 10000.0
N_GROUPS = 4
EXPERTS_PER_GROUP = 8
N_EXPERTS = N_GROUPS * EXPERTS_PER_GROUP
D_EXPERT = 512
RMS_EPS = 1e-6
NEG_INF = -1e30
LAM_INIT = 0.8 - 0.6 * math.exp(-0.3 * 0)
SCALE = HEAD_DIM ** -0.5

C_QD, C_KD, C_VD, C_QS, C_KS, C_VS, C_GATE = 0, 2048, 2560, 3072, 5120, 5632, 6144
IN_COLS = 10240
ATTN_COLS = C_GATE

LANES = 128
PROJ_TN = 512
ROW_BLOCK = 768
ATT_BLOCK = 256
PAGES_PER_STEP = 4
VMEM_LIMIT = 56 * 1024 * 1024


def _cparams(sem):
    return pltpu.CompilerParams(dimension_semantics=sem, vmem_limit_bytes=VMEM_LIMIT)


def _dot(a, b):
    return jnp.dot(a, b, preferred_element_type=F32)


def _dot_nt(a, b):
    return lax.dot_general(a, b, (((1,), (1,)), ((), ())), preferred_element_type=F32)


def _rms(x, g):
    return x * lax.rsqrt(jnp.mean(x * x, axis=-1, keepdims=True) + RMS_EPS) * g


def _rope_table_kernel(pos_ref, inv_ref, cos_ref, sin_ref):
    ang = pos_ref[...] * inv_ref[...]
    lane = lax.broadcasted_iota(jnp.int32, ang.shape, 1)
    s = jnp.sin(ang)
    cos_ref[...] = jnp.cos(ang)
    sin_ref[...] = jnp.where(lane < HEAD_DIM // 2, -s, s)


def _rope_tables(pos, inv, tm):
    t = pos.shape[0]
    return pl.pallas_call(
        _rope_table_kernel,
        grid=(t // tm,),
        in_specs=[pl.BlockSpec((tm, 1), lambda i: (i, 0)), pl.BlockSpec((1, LANES), lambda i: (0, 0))],
        out_specs=[pl.BlockSpec((tm, LANES), lambda i: (i, 0))] * 2,
        out_shape=[jax.ShapeDtypeStruct((t, LANES), F32)] * 2,
        compiler_params=_cparams(("arbitrary",)),
        name="rope_tables",
    )(pos, inv)


N_ROPE_BLOCKS = C_VD // PROJ_TN
N_ATTN_BLOCKS = ATTN_COLS // PROJ_TN


def _proj_kernel(x_ref, g_ref, w_ref, cos_ref, sin_ref, z_ref, zb_ref, h_ref):
    j = pl.program_id(1)

    @pl.when(j == 0)
    def _():
        h_ref[...] = _rms(x_ref[...], g_ref[...]).astype(BF16)

    acc = _dot(h_ref[...], w_ref[...])

    @pl.when(j < N_ROPE_BLOCKS)
    def _():
        cos = cos_ref[...]
        sin = sin_ref[...]
        parts = []
        for c in range(PROJ_TN // HEAD_DIM):
            a = acc[:, c * HEAD_DIM:(c + 1) * HEAD_DIM]
            parts.append(a * cos + pltpu.roll(a, HEAD_DIM // 2, 1) * sin)
        r = jnp.concatenate(parts, axis=1)
        z_ref[...] = r
        zb_ref[...] = r.astype(BF16)

    @pl.when((j >= N_ROPE_BLOCKS) & (j < N_ATTN_BLOCKS))
    def _():
        z_ref[...] = acc
        zb_ref[...] = acc.astype(BF16)

    @pl.when(j >= N_ATTN_BLOCKS)
    def _():
        z_ref[...] = acc


def _project(x, g, w_bf16, cos, sin, tm):
    t = x.shape[0]
    return pl.pallas_call(
        _proj_kernel,
        grid=(t // tm, IN_COLS // PROJ_TN),
        in_specs=[
            pl.BlockSpec((tm, D_MODEL), lambda i, j: (i, 0)),
            pl.BlockSpec((1, D_MODEL), lambda i, j: (0, 0)),
            pl.BlockSpec((D_MODEL, PROJ_TN), lambda i, j: (0, j)),
            pl.BlockSpec((tm, LANES), lambda i, j: (i, 0)),
            pl.BlockSpec((tm, LANES), lambda i, j: (i, 0)),
        ],
        out_specs=[
            pl.BlockSpec((tm, PROJ_TN), lambda i, j: (i, j)),
            pl.BlockSpec((tm, PROJ_TN), lambda i, j: (i, jnp.minimum(j, N_ATTN_BLOCKS - 1))),
        ],
        out_shape=[jax.ShapeDtypeStruct((t, IN_COLS), F32), jax.ShapeDtypeStruct((t, ATTN_COLS), BF16)],
        scratch_shapes=[pltpu.VMEM((tm, D_MODEL), BF16)],
        compiler_params=_cparams(("arbitrary", "arbitrary")),
        name="in_proj",
    )(x, g, w_bf16, cos, sin)


def _lambda(lq1, lk1, lq2, lk2):
    s1 = jnp.sum(lq1[...] * lk1[...], axis=-1, keepdims=True)
    s2 = jnp.sum(lq2[...] * lk2[...], axis=-1, keepdims=True)
    return jnp.exp(s1) - jnp.exp(s2) + LAM_INIT


def _diff_finish(acc1, l1, acc2, l2, lam, subln):
    o = acc1 / l1 - lam * (acc2 / l2)
    o = o * lax.rsqrt(jnp.mean(o * o, axis=-1, keepdims=True) + RMS_EPS)
    return o * subln * (1.0 - LAM_INIT)


def _softmax_step(s, v, m_ref, l_ref, acc_ref, idx):
    m_old = m_ref[idx]
    m_new = jnp.maximum(m_old, jnp.max(s, axis=-1, keepdims=True))
    alpha = jnp.exp(m_old - m_new)
    p = jnp.exp(s - m_new)
    l_ref[idx] = alpha * l_ref[idx] + jnp.sum(p, axis=-1, keepdims=True)
    acc_ref[idx] = alpha * acc_ref[idx] + _dot(p.astype(BF16), v)
    m_ref[idx] = m_new


def _strict_lower(n):
    r = lax.broadcasted_iota(jnp.int32, (n, n), 0)
    c = lax.broadcasted_iota(jnp.int32, (n, n), 1)
    return jnp.where(r > c, 1.0, 0.0).astype(BF16)


def _stick_step(z, v, u, keep, r_ref, acc_ref, idx):
    lk = -(jnp.maximum(z, 0.0) + jnp.log1p(jnp.exp(-jnp.abs(z))))
    lkm = lk if keep is None else jnp.where(keep, lk, 0.0)
    hi = lkm.astype(BF16)
    lo = (lkm - hi.astype(F32)).astype(BF16)
    suffix = _dot(hi, u) + _dot(lo, u)
    a = jnp.exp(z + lk + suffix + r_ref[idx])
    if keep is not None:
        a = jnp.where(keep, a, 0.0)
    acc_ref[idx] = acc_ref[idx] + _dot(a.astype(BF16), v)
    r_ref[idx] = r_ref[idx] + jnp.sum(lkm, axis=-1, keepdims=True)


def _diff_prompt_kernel(lq1, lk1, lq2, lk2, sg_ref, q_ref, k_ref, v_ref, o_ref, m_ref, l_ref, acc_ref):
    qi = pl.program_id(1)
    tb = ATT_BLOCK
    m_ref[...] = jnp.full(m_ref.shape, NEG_INF, F32)
    l_ref[...] = jnp.zeros(l_ref.shape, F32)
    acc_ref[...] = jnp.zeros(acc_ref.shape, F32)

    def block(kb, causal):
        start = pl.multiple_of(kb * tb, tb)
        kblk = k_ref[pl.ds(start, tb), :]
        vblk = v_ref[pl.ds(start, tb), :]
        for r in range(DIFF_REP):
            for m in range(2):
                c0 = (r * 2 + m) * HEAD_DIM
                s = _dot_nt(q_ref[:, c0:c0 + HEAD_DIM], kblk[:, m * HEAD_DIM:(m + 1) * HEAD_DIM]) * SCALE
                if causal is not None:
                    s = jnp.where(causal, s, NEG_INF)
                _softmax_step(s, vblk, m_ref, l_ref, acc_ref, r * 2 + m)

    def body(kb, carry):
        block(kb, None)
        return carry

    lax.fori_loop(0, qi, body, 0)
    row = lax.broadcasted_iota(jnp.int32, (tb, tb), 0)
    col = lax.broadcasted_iota(jnp.int32, (tb, tb), 1)
    block(qi, col <= row)

    lam = _lambda(lq1, lk1, lq2, lk2)
    for r in range(DIFF_REP):
        o = _diff_finish(acc_ref[2 * r], l_ref[2 * r], acc_ref[2 * r + 1], l_ref[2 * r + 1], lam, sg_ref[...])
        o_ref[:, r * DIFF_V_DIM:(r + 1) * DIFF_V_DIM] = o.astype(BF16)


def _lam_specs(nd):
    zero = (lambda *a: (0, 0))
    return [pl.BlockSpec((1, HEAD_DIM), zero)] * 4 + [pl.BlockSpec((1, DIFF_V_DIM), zero)]


def _diff_prompt(zb, lams, subln):
    t = zb.shape[0]
    tb = ATT_BLOCK
    qw = DIFF_REP * 2 * HEAD_DIM
    return pl.pallas_call(
        _diff_prompt_kernel,
        grid=(DIFF_KV_HEADS, t // tb),
        in_specs=_lam_specs(2) + [
            pl.BlockSpec((tb, qw), lambda g, i: (i, g)),
            pl.BlockSpec((t, 2 * HEAD_DIM), lambda g, i: (0, C_KD // (2 * HEAD_DIM) + g)),
            pl.BlockSpec((t, DIFF_V_DIM), lambda g, i: (0, C_VD // DIFF_V_DIM + g)),
        ],
        out_specs=pl.BlockSpec((tb, DIFF_REP * DIFF_V_DIM), lambda g, i: (i, g)),
        out_shape=jax.ShapeDtypeStruct((t, D_MODEL), BF16),
        scratch_shapes=[
            pltpu.VMEM((2 * DIFF_REP, tb, 1), F32),
            pltpu.VMEM((2 * DIFF_REP, tb, 1), F32),
            pltpu.VMEM((2 * DIFF_REP, tb, DIFF_V_DIM), F32),
        ],
        compiler_params=_cparams(("arbitrary", "arbitrary")),
        name="diff_prompt",
    )(*lams, subln, zb, zb, zb)


def _sb_prompt_kernel(q_ref, k_ref, v_ref, o_ref, r_ref, acc_ref):
    qi = pl.program_id(1)
    tb = ATT_BLOCK
    r_ref[...] = jnp.zeros(r_ref.shape, F32)
    acc_ref[...] = jnp.zeros(acc_ref.shape, F32)
    u = _strict_lower(tb)

    def block(kb, keep):
        start = pl.multiple_of(kb * tb, tb)
        kblk = k_ref[pl.ds(start, tb), :]
        vblk = v_ref[pl.ds(start, tb), :]
        for r in range(SB_REP):
            z = _dot_nt(q_ref[:, r * HEAD_DIM:(r + 1) * HEAD_DIM], kblk) * SCALE
            _stick_step(z, vblk, u, keep, r_ref, acc_ref, r)

    row = lax.broadcasted_iota(jnp.int32, (tb, tb), 0)
    col = lax.broadcasted_iota(jnp.int32, (tb, tb), 1)
    block(qi, col < row)

    def body(n, carry):
        block(qi - 1 - n, None)
        return carry

    lax.fori_loop(0, qi, body, 0)
    for r in range(SB_REP):
        o_ref[:, r * HEAD_DIM:(r + 1) * HEAD_DIM] = acc_ref[r].astype(BF16)


def _sb_prompt(zb):
    t = zb.shape[0]
    tb = ATT_BLOCK
    qw = SB_REP * HEAD_DIM
    return pl.pallas_call(
        _sb_prompt_kernel,
        grid=(SB_KV_HEADS, t // tb),
        in_specs=[
            pl.BlockSpec((tb, qw), lambda h, i: (i, C_QS // qw + h)),
            pl.BlockSpec((t, HEAD_DIM), lambda h, i: (0, C_KS // HEAD_DIM + h)),
            pl.BlockSpec((t, HEAD_DIM), lambda h, i: (0, C_VS // HEAD_DIM + h)),
        ],
        out_specs=pl.BlockSpec((tb, qw), lambda h, i: (i, h)),
        out_shape=jax.ShapeDtypeStruct((t, D_MODEL), BF16),
        scratch_shapes=[pltpu.VMEM((SB_REP, tb, 1), F32), pltpu.VMEM((SB_REP, tb, HEAD_DIM), F32)],
        compiler_params=_cparams(("arbitrary", "arbitrary")),
        name="sb_prompt",
    )(zb, zb, zb)


PAGE = 128
CHUNKS = 4
Q_ROWS = 16
DEC_SEQ = 4


def _page_chunk(ref, c):
    return ref[pl.ds(c, PAGE, stride=CHUNKS), :].astype(BF16)


def _decode_kernel(pt_ref, lq1, lk1, lq2, lk2, sg_ref, qd_ref, qs_ref, nkd, nvd, nks, nvs, *rest):
    n_in = 4 * PAGES_PER_STEP
    pages = rest[:n_in]
    od_ref, os_ref = rest[n_in:n_in + 2]
    m_ref, l_ref, accd_ref, r_ref, accs_ref = rest[n_in + 2:]
    s_id = pl.program_id(1)
    u = _strict_lower(PAGE)

    def page(kd, vd, ks, vs, causal, strict):
        for g in range(DIFF_KV_HEADS):
            v = jnp.concatenate([_page_chunk(vd, c * 2 + g) for c in range(2)], axis=1)
            for m in range(2):
                idx = g * 2 + m
                s = _dot_nt(qd_ref[idx], _page_chunk(kd, idx)) * SCALE
                if causal is not None:
                    s = jnp.where(causal, s, NEG_INF)
                _softmax_step(s, v, m_ref, l_ref, accd_ref, idx)
        for h in range(SB_KV_HEADS):
            z = _dot_nt(qs_ref[h], _page_chunk(ks, h)) * SCALE
            _stick_step(z, _page_chunk(vs, h), u, strict, r_ref, accs_ref, h)

    @pl.when(s_id == 0)
    def _():
        m_ref[...] = jnp.full(m_ref.shape, NEG_INF, F32)
        l_ref[...] = jnp.zeros(l_ref.shape, F32)
        accd_ref[...] = jnp.zeros(accd_ref.shape, F32)
        r_ref[...] = jnp.zeros(r_ref.shape, F32)
        accs_ref[...] = jnp.zeros(accs_ref.shape, F32)
        tq = lax.broadcasted_iota(jnp.int32, (Q_ROWS, PAGE), 0) & (DEC_SEQ - 1)
        key = lax.broadcasted_iota(jnp.int32, (Q_ROWS, PAGE), 1)
        page(nkd, nvd, nks, nvs, key <= tq, key < tq)

    for p in range(PAGES_PER_STEP):
        page(*pages[4 * p:4 * p + 4], None, None)

    @pl.when(s_id == pl.num_programs(1) - 1)
    def _():
        lam = _lambda(lq1, lk1, lq2, lk2)
        for g in range(DIFF_KV_HEADS):
            o = _diff_finish(accd_ref[2 * g], l_ref[2 * g], accd_ref[2 * g + 1], l_ref[2 * g + 1], lam, sg_ref[...])
            od_ref[g] = o.astype(BF16)
        for h in range(SB_KV_HEADS):
            os_ref[h] = accs_ref[h].astype(BF16)


def _decode(page_table, lams, subln, qd, qs, new_pages, caches):
    nb, n_pages = page_table.shape
    steps = n_pages // PAGES_PER_STEP
    zero2 = lambda b, s, pt: (0, 0)
    per_b4 = lambda b, s, pt: (b, 0, 0, 0)
    per_b3 = lambda b, s, pt: (b, 0, 0)

    def page_spec(p):
        return pl.BlockSpec((None, PAGE * CHUNKS, LANES),
                            lambda b, s, pt: (pt[b, n_pages - 1 - (s * PAGES_PER_STEP + p)], 0, 0))

    in_specs = ([pl.BlockSpec((1, HEAD_DIM), zero2)] * 4 + [pl.BlockSpec((1, DIFF_V_DIM), zero2)]
                + [pl.BlockSpec((None, 4, Q_ROWS, HEAD_DIM), per_b4)] * 2
                + [pl.BlockSpec((None, PAGE * CHUNKS, LANES), per_b3)] * 4)
    page_args = []
    for p in range(PAGES_PER_STEP):
        in_specs += [page_spec(p)] * 4
        page_args += list(caches)
    grid_spec = pltpu.PrefetchScalarGridSpec(
        num_scalar_prefetch=1,
        grid=(nb, steps),
        in_specs=in_specs,
        out_specs=[pl.BlockSpec((None, DIFF_KV_HEADS, Q_ROWS, DIFF_V_DIM), per_b4),
                   pl.BlockSpec((None, SB_KV_HEADS, Q_ROWS, HEAD_DIM), per_b4)],
        scratch_shapes=[
            pltpu.VMEM((4, Q_ROWS, 1), F32), pltpu.VMEM((4, Q_ROWS, 1), F32),
            pltpu.VMEM((4, Q_ROWS, DIFF_V_DIM), F32),
            pltpu.VMEM((SB_KV_HEADS, Q_ROWS, 1), F32), pltpu.VMEM((SB_KV_HEADS, Q_ROWS, HEAD_DIM), F32),
        ],
    )
    return pl.pallas_call(
        _decode_kernel,
        grid_spec=grid_spec,
        out_shape=[jax.ShapeDtypeStruct((nb, DIFF_KV_HEADS, Q_ROWS, DIFF_V_DIM), BF16),
                   jax.ShapeDtypeStruct((nb, SB_KV_HEADS, Q_ROWS, HEAD_DIM), BF16)],
        compiler_params=_cparams(("arbitrary", "arbitrary")),
        name="decode_attn",
    )(page_table, *lams, subln, qd, qs, *new_pages, *page_args)


def _merge_kernel(od_ref, os_ref, gd_ref, gs_ref, wd_ref, ws_ref, o_ref):
    g_d = 1.0 / (1.0 + jnp.exp(-gd_ref[...]))
    g_s = 1.0 / (1.0 + jnp.exp(-gs_ref[...]))
    o_ref[...] = (g_d * _dot(od_ref[...], wd_ref[...]) + g_s * _dot(os_ref[...], ws_ref[...])).astype(BF16)


def _merge(od, os_, z32, wbd, wbs, tm):
    t = od.shape[0]
    tn = PROJ_TN
    gd0 = C_GATE // tn
    gs0 = (C_GATE + D_MODEL) // tn
    return pl.pallas_call(
        _merge_kernel,
        grid=(t // tm, D_MODEL // tn),
        in_specs=[
            pl.BlockSpec((tm, D_MODEL), lambda i, j: (i, 0)),
            pl.BlockSpec((tm, D_MODEL), lambda i, j: (i, 0)),
            pl.BlockSpec((tm, tn), lambda i, j: (i, gd0 + j)),
            pl.BlockSpec((tm, tn), lambda i, j: (i, gs0 + j)),
            pl.BlockSpec((D_MODEL, tn), lambda i, j: (0, j)),
            pl.BlockSpec((D_MODEL, tn), lambda i, j: (0, j)),
        ],
        out_specs=pl.BlockSpec((tm, tn), lambda i, j: (i, j)),
        out_shape=jax.ShapeDtypeStruct((t, D_MODEL), BF16),
        compiler_params=_cparams(("arbitrary", "arbitrary")),
        name="merge",
    )(od, os_, z32, z32, wbd, wbs)


def _out_kernel(x_ref, m_ref, w_ref, o_ref):
    o_ref[...] = x_ref[...] + _dot(m_ref[...], w_ref[...])


def _out_proj(x, merged, w_out, tm):
    t = x.shape[0]
    tn = PROJ_TN
    return pl.pallas_call(
        _out_kernel,
        grid=(t // tm, D_MODEL // tn),
        in_specs=[
            pl.BlockSpec((tm, tn), lambda i, j: (i, j)),
            pl.BlockSpec((tm, D_MODEL), lambda i, j: (i, 0)),
            pl.BlockSpec((D_MODEL, tn), lambda i, j: (0, j)),
        ],
        out_specs=pl.BlockSpec((tm, tn), lambda i, j: (i, j)),
        out_shape=jax.ShapeDtypeStruct((t, D_MODEL), F32),
        compiler_params=_cparams(("arbitrary", "arbitrary")),
        name="out_proj",
    )(x, merged, w_out)


def _router_kernel(x_ref, g_ref, w_ref, b_ref, hn_ref, comb_ref):
    hn = _rms(x_ref[...], g_ref[...]).astype(BF16)
    hn_ref[...] = hn
    logits = _dot(hn, w_ref[...]) + b_ref[...]
    gl = logits[:, :LANES]
    el = logits[:, LANES:]
    lane = lax.broadcasted_iota(jnp.int32, gl.shape, 1).astype(F32)
    far = float(LANES)

    def first_max(v):
        mx = jnp.max(v, axis=-1, keepdims=True)
        return mx, jnp.min(jnp.where(v == mx, lane, far), axis=-1, keepdims=True)

    gl = jnp.where(lane < N_GROUPS, gl, NEG_INF)
    gmax, grp = first_max(gl)
    p_grp = 1.0 / jnp.sum(jnp.exp(gl - gmax), axis=-1, keepdims=True)
    lo = grp * EXPERTS_PER_GROUP
    es = jnp.where((lane >= lo) & (lane < lo + EXPERTS_PER_GROUP), el, NEG_INF)
    v1, i1 = first_max(es)
    es2 = jnp.where(lane == i1, NEG_INF, es)
    v2, i2 = first_max(es2)
    e2 = jnp.exp(v2 - v1)
    w1 = 1.0 / (1.0 + e2)
    w2 = e2 / (1.0 + e2)
    comb_ref[...] = p_grp * (jnp.where(lane == i1, w1, 0.0) + jnp.where(lane == i2, w2, 0.0))


def _router(x2, g, w_r, b_r, tm):
    t = x2.shape[0]
    return pl.pallas_call(
        _router_kernel,
        grid=(t // tm,),
        in_specs=[
            pl.BlockSpec((tm, D_MODEL), lambda i: (i, 0)),
            pl.BlockSpec((1, D_MODEL), lambda i: (0, 0)),
            pl.BlockSpec((D_MODEL, 2 * LANES), lambda i: (0, 0)),
            pl.BlockSpec((1, 2 * LANES), lambda i: (0, 0)),
        ],
        out_specs=[pl.BlockSpec((tm, D_MODEL), lambda i: (i, 0)), pl.BlockSpec((tm, LANES), lambda i: (i, 0))],
        out_shape=[jax.ShapeDtypeStruct((t, D_MODEL), BF16), jax.ShapeDtypeStruct((t, LANES), F32)],
        compiler_params=_cparams(("arbitrary",)),
        name="router",
    )(x2, g, w_r, b_r)


def _moe_kernel(hn_ref, comb_ref, x_ref, wg_ref, wu_ref, wd_ref, gf_ref, y_ref):
    e = pl.program_id(1)

    @pl.when(e == 0)
    def _():
        y_ref[...] = x_ref[...]

    hn = hn_ref[...]
    lane = lax.broadcasted_iota(jnp.int32, comb_ref.shape, 1)
    c = jnp.sum(jnp.where(lane == e, comb_ref[...], 0.0), axis=-1, keepdims=True)
    gate = _dot(hn, wg_ref[...])
    hid = gate * (1.0 / (1.0 + jnp.exp(-gate))) * _dot(hn, wu_ref[...])
    y_ref[...] += _dot((hid * c).astype(BF16), wd_ref[...])

    @pl.when(e == pl.num_programs(1) - 1)
    def _():
        y_ref[...] = _rms(y_ref[...], gf_ref[...])


def _moe(hn, comb, x2, wg, wu, wd, gf, tm):
    t = hn.shape[0]
    return pl.pallas_call(
        _moe_kernel,
        grid=(t // tm, N_EXPERTS),
        in_specs=[
            pl.BlockSpec((tm, D_MODEL), lambda i, e: (i, 0)),
            pl.BlockSpec((tm, LANES), lambda i, e: (i, 0)),
            pl.BlockSpec((tm, D_MODEL), lambda i, e: (i, 0)),
            pl.BlockSpec((None, D_MODEL, D_EXPERT), lambda i, e: (e, 0, 0)),
            pl.BlockSpec((None, D_MODEL, D_EXPERT), lambda i, e: (e, 0, 0)),
            pl.BlockSpec((None, D_EXPERT, D_MODEL), lambda i, e: (e, 0, 0)),
            pl.BlockSpec((1, D_MODEL), lambda i, e: (0, 0)),
        ],
        out_specs=pl.BlockSpec((tm, D_MODEL), lambda i, e: (i, 0)),
        out_shape=jax.ShapeDtypeStruct((t, D_MODEL), F32),
        compiler_params=_cparams(("arbitrary", "arbitrary")),
        name="moe",
    )(hn, comb, x2, wg, wu, wd, gf)


def _post_attention(x, od, os_, z32, w, tm):
    merged = _merge(od, os_, z32, w["wbd"], w["wbs"], tm)
    x2 = _out_proj(x, merged, w["wout"], tm)
    hn, comb = _router(x2, w["g_ffn"], w["w_r"], w["b_r"], tm)
    return _moe(hn, comb, x2, w["wg"], w["wu"], w["wd"], w["g_final"], tm)


def _pad_rows(a, rows):
    return jnp.pad(a, ((0, rows - a.shape[0]),) + ((0, 0),) * (a.ndim - 1))


def kernel(x_prompt, x_sample, cache_k_diff, cache_v_diff, cache_k_sb, cache_v_sb, page_table, meta_tokens, norm_mix_g, w_in, lambda_q1, lambda_k1, lambda_q2, lambda_k2, subln_g, w_branch_diff, w_branch_sb, w_out, norm_ffn_g, w_router_group, b_router_group, w_router_expert, b_router_expert, w_expert_gate, w_expert_up, w_expert_down, norm_final_g):
    assert x_prompt.shape[0] == 1 and norm_mix_g.shape[0] == 1
    seq = x_prompt.shape[1]
    tp_real = N_META + seq
    tp = -(-tp_real // ROW_BLOCK) * ROW_BLOCK
    assert tp % ATT_BLOCK == 0
    nb, dec_seq, _ = x_sample.shape
    ts = nb * dec_seq
    n_phys = cache_k_diff.shape[1]
    past_len = page_table.shape[1] * PAGE
    assert dec_seq == DEC_SEQ and cache_k_diff.shape[2] == PAGE

    w_in_b = w_in[0].astype(BF16)
    w_r = jnp.zeros((D_MODEL, 2 * LANES), F32)
    w_r = w_r.at[:, :N_GROUPS].set(w_router_group[0]).at[:, LANES:LANES + N_EXPERTS].set(w_router_expert[0])
    b_r = jnp.zeros((1, 2 * LANES), F32)
    b_r = b_r.at[0, :N_GROUPS].set(b_router_group[0]).at[0, LANES:LANES + N_EXPERTS].set(b_router_expert[0])
    w = dict(
        wbd=w_branch_diff[0].astype(BF16), wbs=w_branch_sb[0].astype(BF16), wout=w_out[0].astype(BF16),
        g_ffn=norm_ffn_g, w_r=w_r.astype(BF16), b_r=b_r,
        wg=w_expert_gate[0].astype(BF16), wu=w_expert_up[0].astype(BF16), wd=w_expert_down[0].astype(BF16),
        g_final=norm_final_g.reshape(1, D_MODEL),
    )
    lams = (lambda_q1, lambda_k1, lambda_q2, lambda_k2)

    half = HEAD_DIM // 2
    inv = ROPE_THETA ** (-jnp.arange(half, dtype=F32) * 2.0 / HEAD_DIM)
    inv = jnp.concatenate([inv, inv]).reshape(1, LANES)

    xp = _pad_rows(jnp.concatenate([meta_tokens.astype(F32), x_prompt[0]], axis=0), tp)
    pos_p = jnp.arange(tp, dtype=jnp.int32).astype(F32).reshape(tp, 1)
    cos_p, sin_p = _rope_tables(pos_p, inv, ROW_BLOCK)
    z32_p, zb_p = _project(xp, norm_mix_g, w_in_b, cos_p, sin_p, ROW_BLOCK)
    od_p = _diff_prompt(zb_p, lams, subln_g)
    os_p = _sb_prompt(zb_p)
    y_p = _post_attention(xp, od_p, os_p, z32_p, w, ROW_BLOCK)

    xs = x_sample.reshape(ts, D_MODEL)
    pos_s = jnp.tile(past_len + jnp.arange(dec_seq, dtype=jnp.int32), nb).astype(F32).reshape(ts, 1)
    cos_s, sin_s = _rope_tables(pos_s, inv, ts)
    z32_s, zb_s = _project(xs, norm_mix_g, w_in_b, cos_s, sin_s, ts)

    qd = zb_s[:, C_QD:C_KD].reshape(nb, dec_seq, DIFF_KV_HEADS, DIFF_REP, 2, HEAD_DIM)
    qd = qd.transpose(0, 2, 4, 3, 1, 5).reshape(nb, 4, Q_ROWS, HEAD_DIM)
    qs = zb_s[:, C_QS:C_KS].reshape(nb, dec_seq, SB_KV_HEADS, SB_REP, HEAD_DIM)
    qs = qs.transpose(0, 2, 3, 1, 4).reshape(nb, SB_KV_HEADS, Q_ROWS, HEAD_DIM)

    def new_page(a):
        return jnp.pad(a.reshape(nb, dec_seq * CHUNKS, LANES), ((0, 0), (0, (PAGE - dec_seq) * CHUNKS), (0, 0)))

    nvd = z32_s[:, C_VD:C_QS].reshape(nb, dec_seq, DIFF_KV_HEADS, 2, LANES).transpose(0, 1, 3, 2, 4)
    new_pages = (new_page(z32_s[:, C_KD:C_VD]), new_page(nvd),
                 new_page(z32_s[:, C_KS:C_VS]), new_page(z32_s[:, C_VS:C_GATE]))
    cvd = cache_v_diff[0].reshape(n_phys, PAGE, DIFF_KV_HEADS, 2, LANES).transpose(0, 1, 3, 2, 4)
    caches = (cache_k_diff[0].reshape(n_phys, PAGE * CHUNKS, LANES), cvd.reshape(n_phys, PAGE * CHUNKS, LANES),
              cache_k_sb[0].reshape(n_phys, PAGE * CHUNKS, LANES), cache_v_sb[0].reshape(n_phys, PAGE * CHUNKS, LANES))
    od_s, os_s = _decode(page_table, lams, subln_g, qd, qs, new_pages, caches)
    od_s = od_s.reshape(nb, DIFF_KV_HEADS, DIFF_REP, dec_seq, DIFF_V_DIM).transpose(0, 3, 1, 2, 4).reshape(ts, D_MODEL)
    os_s = os_s.reshape(nb, SB_KV_HEADS, SB_REP, dec_seq, HEAD_DIM).transpose(0, 3, 1, 2, 4).reshape(ts, D_MODEL)
    y_s = _post_attention(xs, od_s, os_s, z32_s, w, ts)

    def kv(z, rows, lead):
        return (z[:rows, C_KD:C_VD].reshape(lead + (DIFF_KV_HEADS, 2, HEAD_DIM)),
                z[:rows, C_VD:C_QS].reshape(lead + (DIFF_KV_HEADS, DIFF_V_DIM)),
                z[:rows, C_KS:C_VS].reshape(lead + (SB_KV_HEADS, HEAD_DIM)),
                z[:rows, C_VS:C_GATE].reshape(lead + (SB_KV_HEADS, HEAD_DIM)))

    y_prompt = y_p[N_META:tp_real].reshape(1, seq, D_MODEL)
    y_sample = y_s.reshape(nb, dec_seq, D_MODEL)
    return (y_prompt, y_sample) + kv(z32_p, tp_real, (1, 1, tp_real)) + kv(z32_s, ts, (1, nb, dec_seq))
```

```python
import functools
import math

import jax
import jax.numpy as jnp
import numpy as np
from jax import lax
from jax.experimental import pallas as pl
from jax.experimental.pallas import tpu as pltpu

F32 = jnp.float32
BF16 = jnp.bfloat16

D_MODEL = 2048
N_META = 16
HEAD_DIM = 128
DIFF_KV_HEADS = 2
DIFF_REP = 4
DIFF_V_DIM = 2 * HEAD_DIM
SB_KV_HEADS = 4
SB_REP = 4
ROPE_THETA = 10000.0
N_GROUPS = 4
EXPERTS_PER_GROUP = 8
N_EXPERTS = N_GROUPS * EXPERTS_PER_GROUP
D_EXPERT = 512
RMS_EPS = 1e-6
NEG_INF = -1e30
LAM_INIT = 0.8 - 0.6 * math.exp(-0.3 * 0)
SCALE = HEAD_DIM ** -0.5

C_QD, C_KD, C_VD, C_QS, C_KS, C_VS, C_GATE = 0, 2048, 2560, 3072, 5120, 5632, 6144
IN_COLS = 10240
ATTN_COLS = C_GATE

LANES = 128
PROJ_TN = 512
ROW_BLOCK = 768
ATT_BLOCK = 256
PAGES_PER_STEP = 8
VMEM_LIMIT = 56 * 1024 * 1024


def _cparams(sem):
    return pltpu.CompilerParams(dimension_semantics=sem, vmem_limit_bytes=VMEM_LIMIT)


def _dot(a, b):
    return jnp.dot(a, b, preferred_element_type=F32)


def _dot_nt(a, b):
    return lax.dot_general(a, b, (((1,), (1,)), ((), ())), preferred_element_type=F32)


def _rms(x, g):
    return x * lax.rsqrt(jnp.mean(x * x, axis=-1, keepdims=True) + RMS_EPS) * g


def _rope_table_kernel(pos_ref, inv_ref, cos_ref, sin_ref):
    ang = pos_ref[...] * inv_ref[...]
    lane = lax.broadcasted_iota(jnp.int32, ang.shape, 1)
    s = jnp.sin(ang)
    cos_ref[...] = jnp.cos(ang)
    sin_ref[...] = jnp.where(lane < HEAD_DIM // 2, -s, s)


def _rope_tables(pos, inv, tm):
    t = pos.shape[0]
    return pl.pallas_call(
        _rope_table_kernel,
        grid=(t // tm,),
        in_specs=[pl.BlockSpec((tm, 1), lambda i: (i, 0)), pl.BlockSpec((1, LANES), lambda i: (0, 0))],
        out_specs=[pl.BlockSpec((tm, LANES), lambda i: (i, 0))] * 2,
        out_shape=[jax.ShapeDtypeStruct((t, LANES), F32)] * 2,
        compiler_params=_cparams(("arbitrary",)),
        name="rope_tables",
    )(pos, inv)


N_ROPE_BLOCKS = C_VD // PROJ_TN
N_ATTN_BLOCKS = ATTN_COLS // PROJ_TN


def _proj_kernel(x_ref, g_ref, w_ref, cos_ref, sin_ref, z_ref, zb_ref, h_ref):
    j = pl.program_id(1)

    @pl.when(j == 0)
    def _():
        h_ref[...] = _rms(x_ref[...], g_ref[...]).astype(BF16)

    acc = _dot(h_ref[...], w_ref[...])

    @pl.when(j < N_ROPE_BLOCKS)
    def _():
        cos = cos_ref[...]
        sin = sin_ref[...]
        parts = []
        for c in range(PROJ_TN // HEAD_DIM):
            a = acc[:, c * HEAD_DIM:(c + 1) * HEAD_DIM]
            parts.append(a * cos + pltpu.roll(a, HEAD_DIM // 2, 1) * sin)
        r = jnp.concatenate(parts, axis=1)
        z_ref[...] = r
        zb_ref[...] = r.astype(BF16)

    @pl.when((j >= N_ROPE_BLOCKS) & (j < N_ATTN_BLOCKS))
    def _():
        z_ref[...] = acc
        zb_ref[...] = acc.astype(BF16)

    @pl.when(j >= N_ATTN_BLOCKS)
    def _():
        z_ref[...] = acc


def _project(x, g, w_bf16, cos, sin, tm):
    t = x.shape[0]
    return pl.pallas_call(
        _proj_kernel,
        grid=(t // tm, IN_COLS // PROJ_TN),
        in_specs=[
            pl.BlockSpec((tm, D_MODEL), lambda i, j: (i, 0)),
            pl.BlockSpec((1, D_MODEL), lambda i, j: (0, 0)),
            pl.BlockSpec((D_MODEL, PROJ_TN), lambda i, j: (0, j)),
            pl.BlockSpec((tm, LANES), lambda i, j: (i, 0)),
            pl.BlockSpec((tm, LANES), lambda i, j: (i, 0)),
        ],
        out_specs=[
            pl.BlockSpec((tm, PROJ_TN), lambda i, j: (i, j)),
            pl.BlockSpec((tm, PROJ_TN), lambda i, j: (i, jnp.minimum(j, N_ATTN_BLOCKS - 1))),
        ],
        out_shape=[jax.ShapeDtypeStruct((t, IN_COLS), F32), jax.ShapeDtypeStruct((t, ATTN_COLS), BF16)],
        scratch_shapes=[pltpu.VMEM((tm, D_MODEL), BF16)],
        compiler_params=_cparams(("arbitrary", "arbitrary")),
        name="in_proj",
    )(x, g, w_bf16, cos, sin)


def _lambda(lq1, lk1, lq2, lk2):
    s1 = jnp.sum(lq1[...] * lk1[...], axis=-1, keepdims=True)
    s2 = jnp.sum(lq2[...] * lk2[...], axis=-1, keepdims=True)
    return jnp.exp(s1) - jnp.exp(s2) + LAM_INIT


def _diff_finish(acc1, l1, acc2, l2, lam, subln):
    w = acc1.shape[1]
    o = acc1 / _lanes(l1, w) - lam * (acc2 / _lanes(l2, w))
    o = o * lax.rsqrt(jnp.mean(o * o, axis=-1, keepdims=True) + RMS_EPS)
    return o * subln * (1.0 - LAM_INIT)


def _lanes(x, width):
    return x if width == LANES else jnp.concatenate([x] * (width // LANES), axis=1)


def _softmax_step(s, v, m_ref, l_ref, acc_ref, idx):
    m_old = m_ref[idx]
    m_new = jnp.maximum(m_old, jnp.max(s, axis=-1, keepdims=True))
    alpha = jnp.exp(m_old - m_new)
    p = jnp.exp(s - _lanes(m_new, s.shape[1]))
    l_ref[idx] = alpha * l_ref[idx] + jnp.sum(p, axis=-1, keepdims=True)
    acc_ref[idx] = _lanes(alpha, v.shape[1]) * acc_ref[idx] + _dot(p.astype(BF16), v)
    m_ref[idx] = m_new


def _strict_lower(n):
    r = lax.broadcasted_iota(jnp.int32, (n, n), 0)
    c = lax.broadcasted_iota(jnp.int32, (n, n), 1)
    return jnp.where(r > c, 1.0, 0.0).astype(BF16)


def _log_keep(z):
    return -(jnp.maximum(z, 0.0) + jnp.log1p(jnp.exp(-jnp.abs(z))))


def _split_bf16(x):
    hi = x.astype(BF16)
    return hi, (x - hi.astype(F32)).astype(BF16)


SB_CUTOFF = -104.0


def _stick_step(z, v, u, keep, r_ref, acc_ref, idx):
    lk = _log_keep(z)
    lkm = lk if keep is None else jnp.where(keep, lk, 0.0)
    hi, lo = _split_bf16(lkm)
    suffix = _dot(hi, u) + _dot(lo, u)
    r = r_ref[idx]
    a = jnp.exp(z + lk + suffix + _lanes(r, z.shape[1]))
    if keep is not None:
        a = jnp.where(keep, a, 0.0)
    acc_ref[idx] = acc_ref[idx] + _dot(a.astype(BF16), v)
    r_ref[idx] = r + jnp.sum(lkm, axis=-1, keepdims=True)


def _diff_prompt_kernel(lq1, lk1, lq2, lk2, sg_ref, q_ref, k_ref, v_ref, o_ref, m_ref, l_ref, acc_ref):
    qi = pl.program_id(1)
    tb = ATT_BLOCK
    m_ref[...] = jnp.full(m_ref.shape, NEG_INF, F32)
    l_ref[...] = jnp.zeros(l_ref.shape, F32)
    acc_ref[...] = jnp.zeros(acc_ref.shape, F32)

    def block(kb, causal):
        start = pl.multiple_of(kb * tb, tb)
        kblk = k_ref[pl.ds(start, tb), :]
        vblk = v_ref[pl.ds(start, tb), :]
        for r in range(DIFF_REP):
            for m in range(2):
                c0 = (r * 2 + m) * HEAD_DIM
                s = _dot_nt(q_ref[:, c0:c0 + HEAD_DIM], kblk[:, m * HEAD_DIM:(m + 1) * HEAD_DIM]) * SCALE
                if causal is not None:
                    s = jnp.where(causal, s, NEG_INF)
                _softmax_step(s, vblk, m_ref, l_ref, acc_ref, r * 2 + m)

    def body(kb, carry):
        block(kb, None)
        return carry

    lax.fori_loop(0, qi, body, 0)
    row = lax.broadcasted_iota(jnp.int32, (tb, tb), 0)
    col = lax.broadcasted_iota(jnp.int32, (tb, tb), 1)
    block(qi, col <= row)

    lam = _lambda(lq1, lk1, lq2, lk2)
    for r in range(DIFF_REP):
        o = _diff_finish(acc_ref[2 * r], l_ref[2 * r], acc_ref[2 * r + 1], l_ref[2 * r + 1], lam, sg_ref[...])
        o_ref[:, r * DIFF_V_DIM:(r + 1) * DIFF_V_DIM] = o.astype(BF16)


def _lam_specs(nd):
    zero = (lambda *a: (0, 0))
    return [pl.BlockSpec((1, HEAD_DIM), zero)] * 4 + [pl.BlockSpec((1, DIFF_V_DIM), zero)]


def _diff_prompt(zb, lams, subln):
    t = zb.shape[0]
    tb = ATT_BLOCK
    qw = DIFF_REP * 2 * HEAD_DIM
    return pl.pallas_call(
        _diff_prompt_kernel,
        grid=(DIFF_KV_HEADS, t // tb),
        in_specs=_lam_specs(2) + [
            pl.BlockSpec((tb, qw), lambda g, i: (i, g)),
            pl.BlockSpec((t, 2 * HEAD_DIM), lambda g, i: (0, C_KD // (2 * HEAD_DIM) + g)),
            pl.BlockSpec((t, DIFF_V_DIM), lambda g, i: (0, C_VD // DIFF_V_DIM + g)),
        ],
        out_specs=pl.BlockSpec((tb, DIFF_REP * DIFF_V_DIM), lambda g, i: (i, g)),
        out_shape=jax.ShapeDtypeStruct((t, D_MODEL), BF16),
        scratch_shapes=[
            pltpu.VMEM((2 * DIFF_REP, tb, LANES), F32),
            pltpu.VMEM((2 * DIFF_REP, tb, LANES), F32),
            pltpu.VMEM((2 * DIFF_REP, tb, DIFF_V_DIM), F32),
        ],
        compiler_params=_cparams(("arbitrary", "arbitrary")),
        name="diff_prompt",
    )(*lams, subln, zb, zb, zb)


def _sb_prompt_kernel(q_ref, k_ref, v_ref, o_ref, r_ref, acc_ref):
    qi = pl.program_id(1)
    tb = ATT_BLOCK
    r_ref[...] = jnp.zeros(r_ref.shape, F32)
    acc_ref[...] = jnp.zeros(acc_ref.shape, F32)
    u = _strict_lower(tb)

    def block(kb, keep):
        start = pl.multiple_of(kb * tb, tb)
        kblk = k_ref[pl.ds(start, tb), :]
        vblk = v_ref[pl.ds(start, tb), :]
        for r in range(SB_REP):
            z = _dot_nt(q_ref[:, r * HEAD_DIM:(r + 1) * HEAD_DIM], kblk) * SCALE
            _stick_step(z, vblk, u, keep, r_ref, acc_ref, r)

    row = lax.broadcasted_iota(jnp.int32, (tb, tb), 0)
    col = lax.broadcasted_iota(jnp.int32, (tb, tb), 1)
    block(qi, col < row)

    def live(c):
        return (c[0] >= 0) & (c[1] > SB_CUTOFF)

    def body(c):
        block(c[0], None)
        return c[0] - 1, jnp.max(r_ref[...])

    lax.while_loop(live, body, (qi - 1, jnp.max(r_ref[...])))
    for r in range(SB_REP):
        o_ref[:, r * HEAD_DIM:(r + 1) * HEAD_DIM] = acc_ref[r].astype(BF16)


def _sb_prompt(zb):
    t = zb.shape[0]
    tb = ATT_BLOCK
    qw = SB_REP * HEAD_DIM
    return pl.pallas_call(
        _sb_prompt_kernel,
        grid=(SB_KV_HEADS, t // tb),
        in_specs=[
            pl.BlockSpec((tb, qw), lambda h, i: (i, C_QS // qw + h)),
            pl.BlockSpec((t, HEAD_DIM), lambda h, i: (0, C_KS // HEAD_DIM + h)),
            pl.BlockSpec((t, HEAD_DIM), lambda h, i: (0, C_VS // HEAD_DIM + h)),
        ],
        out_specs=pl.BlockSpec((tb, qw), lambda h, i: (i, h)),
        out_shape=jax.ShapeDtypeStruct((t, D_MODEL), BF16),
        scratch_shapes=[pltpu.VMEM((SB_REP, tb, LANES), F32), pltpu.VMEM((SB_REP, tb, HEAD_DIM), F32)],
        compiler_params=_cparams(("arbitrary", "arbitrary")),
        name="sb_prompt",
    )(zb, zb, zb)


PAGE = 128
CHUNKS = 4
Q_ROWS = 16
DEC_SEQ = 4


def _page_chunk(ref, c):
    return ref[pl.ds(c, PAGE, stride=CHUNKS), :].astype(BF16)


def _decode_diff(qd_ref, kds, vds, causal, m_ref, l_ref, acc_ref):
    n = len(kds)
    for g in range(DIFF_KV_HEADS):
        v = [[_page_chunk(vd, c * 2 + g) for c in range(2)] for vd in vds]
        for m in range(2):
            idx = g * 2 + m
            q = qd_ref[idx]
            s = [_dot_nt(q, _page_chunk(kd, idx)) * SCALE for kd in kds]
            if causal is not None:
                s = [jnp.where(causal, x, NEG_INF) for x in s]
            m_old = m_ref[idx]
            m_new = jnp.maximum(m_old, jnp.max(functools.reduce(jnp.maximum, s), axis=-1, keepdims=True))
            alpha = jnp.exp(m_old - m_new)
            p = [jnp.exp(x - m_new) for x in s]
            l_ref[idx] = alpha * l_ref[idx] + jnp.sum(functools.reduce(jnp.add, p), axis=-1, keepdims=True)
            pb = [x.astype(BF16) for x in p]
            pv = [functools.reduce(jnp.add, [_dot(pb[i], v[i][c]) for i in range(n)]) for c in range(2)]
            acc_ref[idx] = _lanes(alpha, DIFF_V_DIM) * acc_ref[idx] + jnp.concatenate(pv, axis=1)
            m_ref[idx] = m_new


def _decode_stick(qs_ref, kss, vss, strict, u, r_ref, acc_ref):
    n = len(kss)
    rows = n * Q_ROWS
    for h in range(SB_KV_HEADS):
        q = qs_ref[h]
        z = jnp.concatenate([_dot_nt(q, _page_chunk(ks, h)) for ks in kss], axis=0) * SCALE
        lk = _log_keep(z)
        keep = None if strict is None else jnp.concatenate([strict] * n, axis=0)
        lkm = lk if keep is None else jnp.where(keep, lk, 0.0)
        hi, lo = _split_bf16(lkm)
        s2 = _dot(jnp.concatenate([hi, lo], axis=0), u)
        suffix = s2[:rows] + s2[rows:]
        tot = jnp.sum(lkm, axis=-1, keepdims=True)
        r = r_ref[h]
        offs = []
        for i in range(n):
            offs.append(r)
            r = r + tot[i * Q_ROWS:(i + 1) * Q_ROWS]
        a = jnp.exp(z + lk + suffix + jnp.concatenate(offs, axis=0))
        if keep is not None:
            a = jnp.where(keep, a, 0.0)
        ab = a.astype(BF16)
        pv = [_dot(ab[i * Q_ROWS:(i + 1) * Q_ROWS], _page_chunk(vss[i], h)) for i in range(n)]
        acc_ref[h] = acc_ref[h] + functools.reduce(jnp.add, pv)
        r_ref[h] = r


def _decode_kernel(pt_ref, lq1, lk1, lq2, lk2, sg_ref, qd_ref, qs_ref, nkd, nvd, nks, nvs, *rest):
    n_in = 4 * PAGES_PER_STEP
    pages = rest[:n_in]
    od_ref, os_ref = rest[n_in:n_in + 2]
    m_ref, l_ref, accd_ref, r_ref, accs_ref = rest[n_in + 2:]
    s_id = pl.program_id(1)
    u = _strict_lower(PAGE)

    @pl.when(s_id == 0)
    def _():
        m_ref[...] = jnp.full(m_ref.shape, NEG_INF, F32)
        l_ref[...] = jnp.zeros(l_ref.shape, F32)
        accd_ref[...] = jnp.zeros(accd_ref.shape, F32)
        r_ref[...] = jnp.zeros(r_ref.shape, F32)
        accs_ref[...] = jnp.zeros(accs_ref.shape, F32)
        tq = lax.broadcasted_iota(jnp.int32, (Q_ROWS, PAGE), 0) & (DEC_SEQ - 1)
        key = lax.broadcasted_iota(jnp.int32, (Q_ROWS, PAGE), 1)
        _decode_diff(qd_ref, [nkd], [nvd], key <= tq, m_ref, l_ref, accd_ref)
        _decode_stick(qs_ref, [nks], [nvs], key < tq, u, r_ref, accs_ref)

    _decode_diff(qd_ref, pages[0::4], pages[1::4], None, m_ref, l_ref, accd_ref)

    @pl.when(jnp.max(r_ref[...]) > SB_CUTOFF)
    def _():
        _decode_stick(qs_ref, pages[2::4], pages[3::4], None, u, r_ref, accs_ref)

    @pl.when(s_id == pl.num_programs(1) - 1)
    def _():
        lam = _lambda(lq1, lk1, lq2, lk2)
        for g in range(DIFF_KV_HEADS):
            o = _diff_finish(accd_ref[2 * g], l_ref[2 * g], accd_ref[2 * g + 1], l_ref[2 * g + 1], lam, sg_ref[...])
            od_ref[g] = o.astype(BF16)
        for h in range(SB_KV_HEADS):
            os_ref[h] = accs_ref[h].astype(BF16)


def _decode(page_table, lams, subln, qd, qs, new_pages, caches):
    nb, n_pages = page_table.shape
    steps = n_pages // PAGES_PER_STEP
    zero2 = lambda b, s, pt: (0, 0)
    per_b4 = lambda b, s, pt: (b, 0, 0, 0)
    per_b3 = lambda b, s, pt: (b, 0, 0)

    def page_spec(p):
        return pl.BlockSpec((None, PAGE * CHUNKS, LANES),
                            lambda b, s, pt: (pt[b, n_pages - 1 - (s * PAGES_PER_STEP + p)], 0, 0))

    in_specs = ([pl.BlockSpec((1, HEAD_DIM), zero2)] * 4 + [pl.BlockSpec((1, DIFF_V_DIM), zero2)]
                + [pl.BlockSpec((None, 4, Q_ROWS, HEAD_DIM), per_b4)] * 2
                + [pl.BlockSpec((None, PAGE * CHUNKS, LANES), per_b3)] * 4)
    page_args = []
    for p in range(PAGES_PER_STEP):
        in_specs += [page_spec(p)] * 4
        page_args += list(caches)
    grid_spec = pltpu.PrefetchScalarGridSpec(
        num_scalar_prefetch=1,
        grid=(nb, steps),
        in_specs=in_specs,
        out_specs=[pl.BlockSpec((None, DIFF_KV_HEADS, Q_ROWS, DIFF_V_DIM), per_b4),
                   pl.BlockSpec((None, SB_KV_HEADS, Q_ROWS, HEAD_DIM), per_b4)],
        scratch_shapes=[
            pltpu.VMEM((4, Q_ROWS, LANES), F32), pltpu.VMEM((4, Q_ROWS, LANES), F32),
            pltpu.VMEM((4, Q_ROWS, DIFF_V_DIM), F32),
            pltpu.VMEM((SB_KV_HEADS, Q_ROWS, LANES), F32), pltpu.VMEM((SB_KV_HEADS, Q_ROWS, HEAD_DIM), F32),
        ],
    )
    return pl.pallas_call(
        _decode_kernel,
        grid_spec=grid_spec,
        out_shape=[jax.ShapeDtypeStruct((nb, DIFF_KV_HEADS, Q_ROWS, DIFF_V_DIM), BF16),
                   jax.ShapeDtypeStruct((nb, SB_KV_HEADS, Q_ROWS, HEAD_DIM), BF16)],
        compiler_params=_cparams(("arbitrary", "arbitrary")),
        name="decode_attn",
    )(page_table, *lams, subln, qd, qs, *new_pages, *page_args)


def _merge_kernel(od_ref, os_ref, gd_ref, gs_ref, wd_ref, ws_ref, o_ref):
    g_d = 1.0 / (1.0 + jnp.exp(-gd_ref[...]))
    g_s = 1.0 / (1.0 + jnp.exp(-gs_ref[...]))
    o_ref[...] = (g_d * _dot(od_ref[...], wd_ref[...]) + g_s * _dot(os_ref[...], ws_ref[...])).astype(BF16)


def _merge(od, os_, z32, wbd, wbs, tm):
    t = od.shape[0]
    tn = PROJ_TN
    gd0 = C_GATE // tn
    gs0 = (C_GATE + D_MODEL) // tn
    return pl.pallas_call(
        _merge_kernel,
        grid=(t // tm, D_MODEL // tn),
        in_specs=[
            pl.BlockSpec((tm, D_MODEL), lambda i, j: (i, 0)),
            pl.BlockSpec((tm, D_MODEL), lambda i, j: (i, 0)),
            pl.BlockSpec((tm, tn), lambda i, j: (i, gd0 + j)),
            pl.BlockSpec((tm, tn), lambda i, j: (i, gs0 + j)),
            pl.BlockSpec((D_MODEL, tn), lambda i, j: (0, j)),
            pl.BlockSpec((D_MODEL, tn), lambda i, j: (0, j)),
        ],
        out_specs=pl.BlockSpec((tm, tn), lambda i, j: (i, j)),
        out_shape=jax.ShapeDtypeStruct((t, D_MODEL), BF16),
        compiler_params=_cparams(("arbitrary", "arbitrary")),
        name="merge",
    )(od, os_, z32, z32, wbd, wbs)


def _out_kernel(x_ref, m_ref, w_ref, o_ref):
    o_ref[...] = x_ref[...] + _dot(m_ref[...], w_ref[...])


def _out_proj(x, merged, w_out, tm):
    t = x.shape[0]
    tn = PROJ_TN
    return pl.pallas_call(
        _out_kernel,
        grid=(t // tm, D_MODEL // tn),
        in_specs=[
            pl.BlockSpec((tm, tn), lambda i, j: (i, j)),
            pl.BlockSpec((tm, D_MODEL), lambda i, j: (i, 0)),
            pl.BlockSpec((D_MODEL, tn), lambda i, j: (0, j)),
        ],
        out_specs=pl.BlockSpec((tm, tn), lambda i, j: (i, j)),
        out_shape=jax.ShapeDtypeStruct((t, D_MODEL), F32),
        compiler_params=_cparams(("arbitrary", "arbitrary")),
        name="out_proj",
    )(x, merged, w_out)


def _router_kernel(x_ref, g_ref, w_ref, b_ref, hn_ref, comb_ref):
    hn = _rms(x_ref[...], g_ref[...]).astype(BF16)
    hn_ref[...] = hn
    logits = _dot(hn, w_ref[...]) + b_ref[...]
    gl = logits[:, :LANES]
    el = logits[:, LANES:]
    lane = lax.broadcasted_iota(jnp.int32, gl.shape, 1).astype(F32)
    far = float(LANES)

    def first_max(v):
        mx = jnp.max(v, axis=-1, keepdims=True)
        return mx, jnp.min(jnp.where(v == mx, lane, far), axis=-1, keepdims=True)

    gl = jnp.where(lane < N_GROUPS, gl, NEG_INF)
    gmax, grp = first_max(gl)
    p_grp = 1.0 / jnp.sum(jnp.exp(gl - gmax), axis=-1, keepdims=True)
    lo = grp * EXPERTS_PER_GROUP
    es = jnp.where((lane >= lo) & (lane < lo + EXPERTS_PER_GROUP), el, NEG_INF)
    v1, i1 = first_max(es)
    es2 = jnp.where(lane == i1, NEG_INF, es)
    v2, i2 = first_max(es2)
    e2 = jnp.exp(v2 - v1)
    w1 = 1.0 / (1.0 + e2)
    w2 = e2 / (1.0 + e2)
    comb_ref[...] = p_grp * (jnp.where(lane == i1, w1, 0.0) + jnp.where(lane == i2, w2, 0.0))


def _router(x2, g, w_r, b_r, tm):
    t = x2.shape[0]
    return pl.pallas_call(
        _router_kernel,
        grid=(t // tm,),
        in_specs=[
            pl.BlockSpec((tm, D_MODEL), lambda i: (i, 0)),
            pl.BlockSpec((1, D_MODEL), lambda i: (0, 0)),
            pl.BlockSpec((D_MODEL, 2 * LANES), lambda i: (0, 0)),
            pl.BlockSpec((1, 2 * LANES), lambda i: (0, 0)),
        ],
        out_specs=[pl.BlockSpec((tm, D_MODEL), lambda i: (i, 0)), pl.BlockSpec((tm, LANES), lambda i: (i, 0))],
        out_shape=[jax.ShapeDtypeStruct((t, D_MODEL), BF16), jax.ShapeDtypeStruct((t, LANES), F32)],
        compiler_params=_cparams(("arbitrary",)),
        name="router",
    )(x2, g, w_r, b_r)


def _moe_kernel(hn_ref, comb_ref, x_ref, wg_ref, wu_ref, wd_ref, gf_ref, y_ref):
    e = pl.program_id(1)

    @pl.when(e == 0)
    def _():
        y_ref[...] = x_ref[...]

    hn = hn_ref[...]
    lane = lax.broadcasted_iota(jnp.int32, comb_ref.shape, 1)
    c = jnp.sum(jnp.where(lane == e, comb_ref[...], 0.0), axis=-1, keepdims=True)
    gate = _dot(hn, wg_ref[...])
    hid = gate * (1.0 / (1.0 + jnp.exp(-gate))) * _dot(hn, wu_ref[...])
    y_ref[...] += _dot((hid * c).astype(BF16), wd_ref[...])

    @pl.when(e == pl.num_programs(1) - 1)
    def _():
        y_ref[...] = _rms(y_ref[...], gf_ref[...])


def _moe(hn, comb, x2, wg, wu, wd, gf, tm):
    t = hn.shape[0]
    return pl.pallas_call(
        _moe_kernel,
        grid=(t // tm, N_EXPERTS),
        in_specs=[
            pl.BlockSpec((tm, D_MODEL), lambda i, e: (i, 0)),
            pl.BlockSpec((tm, LANES), lambda i, e: (i, 0)),
            pl.BlockSpec((tm, D_MODEL), lambda i, e: (i, 0)),
            pl.BlockSpec((None, D_MODEL, D_EXPERT), lambda i, e: (e, 0, 0)),
            pl.BlockSpec((None, D_MODEL, D_EXPERT), lambda i, e: (e, 0, 0)),
            pl.BlockSpec((None, D_EXPERT, D_MODEL), lambda i, e: (e, 0, 0)),
            pl.BlockSpec((1, D_MODEL), lambda i, e: (0, 0)),
        ],
        out_specs=pl.BlockSpec((tm, D_MODEL), lambda i, e: (i, 0)),
        out_shape=jax.ShapeDtypeStruct((t, D_MODEL), F32),
        compiler_params=_cparams(("arbitrary", "arbitrary")),
        name="moe",
    )(hn, comb, x2, wg, wu, wd, gf)


def _post_attention(x, od, os_, z32, w, tm):
    merged = _merge(od, os_, z32, w["wbd"], w["wbs"], tm)
    x2 = _out_proj(x, merged, w["wout"], tm)
    hn, comb = _router(x2, w["g_ffn"], w["w_r"], w["b_r"], tm)
    return _moe(hn, comb, x2, w["wg"], w["wu"], w["wd"], w["g_final"], tm)


def _pad_rows(a, rows):
    return jnp.pad(a, ((0, rows - a.shape[0]),) + ((0, 0),) * (a.ndim - 1))


def kernel(x_prompt, x_sample, cache_k_diff, cache_v_diff, cache_k_sb, cache_v_sb, page_table, meta_tokens, norm_mix_g, w_in, lambda_q1, lambda_k1, lambda_q2, lambda_k2, subln_g, w_branch_diff, w_branch_sb, w_out, norm_ffn_g, w_router_group, b_router_group, w_router_expert, b_router_expert, w_expert_gate, w_expert_up, w_expert_down, norm_final_g):
    assert x_prompt.shape[0] == 1 and norm_mix_g.shape[0] == 1
    seq = x_prompt.shape[1]
    tp_real = N_META + seq
    tp = -(-tp_real // ROW_BLOCK) * ROW_BLOCK
    assert tp % ATT_BLOCK == 0
    nb, dec_seq, _ = x_sample.shape
    ts = nb * dec_seq
    n_phys = cache_k_diff.shape[1]
    past_len = page_table.shape[1] * PAGE
    assert dec_seq == DEC_SEQ and cache_k_diff.shape[2] == PAGE

    w_in_b = w_in[0].astype(BF16)
    w_r = jnp.zeros((D_MODEL, 2 * LANES), F32)
    w_r = w_r.at[:, :N_GROUPS].set(w_router_group[0]).at[:, LANES:LANES + N_EXPERTS].set(w_router_expert[0])
    b_r = jnp.zeros((1, 2 * LANES), F32)
    b_r = b_r.at[0, :N_GROUPS].set(b_router_group[0]).at[0, LANES:LANES + N_EXPERTS].set(b_router_expert[0])
    w = dict(
        wbd=w_branch_diff[0].astype(BF16), wbs=w_branch_sb[0].astype(BF16), wout=w_out[0].astype(BF16),
        g_ffn=norm_ffn_g, w_r=w_r.astype(BF16), b_r=b_r,
        wg=w_expert_gate[0].astype(BF16), wu=w_expert_up[0].astype(BF16), wd=w_expert_down[0].astype(BF16),
        g_final=norm_final_g.reshape(1, D_MODEL),
    )
    lams = (lambda_q1, lambda_k1, lambda_q2, lambda_k2)

    half = HEAD_DIM // 2
    inv = ROPE_THETA ** (-jnp.arange(half, dtype=F32) * 2.0 / HEAD_DIM)
    inv = jnp.concatenate([inv, inv]).reshape(1, LANES)

    xp = _pad_rows(jnp.concatenate([meta_tokens.astype(F32), x_prompt[0]], axis=0), tp)
    pos_p = jnp.arange(tp, dtype=jnp.int32).astype(F32).reshape(tp, 1)
    cos_p, sin_p = _rope_tables(pos_p, inv, ROW_BLOCK)
    z32_p, zb_p = _project(xp, norm_mix_g, w_in_b, cos_p, sin_p, ROW_BLOCK)
    od_p = _diff_prompt(zb_p, lams, subln_g)
    os_p = _sb_prompt(zb_p)
    y_p = _post_attention(xp, od_p, os_p, z32_p, w, ROW_BLOCK)

    xs = x_sample.reshape(ts, D_MODEL)
    pos_s = jnp.tile(past_len + jnp.arange(dec_seq, dtype=jnp.int32), nb).astype(F32).reshape(ts, 1)
    cos_s, sin_s = _rope_tables(pos_s, inv, ts)
    z32_s, zb_s = _project(xs, norm_mix_g, w_in_b, cos_s, sin_s, ts)

    qd = zb_s[:, C_QD:C_KD].reshape(nb, dec_seq, DIFF_KV_HEADS, DIFF_REP, 2, HEAD_DIM)
    qd = qd.transpose(0, 2, 4, 3, 1, 5).reshape(nb, 4, Q_ROWS, HEAD_DIM)
    qs = zb_s[:, C_QS:C_KS].reshape(nb, dec_seq, SB_KV_HEADS, SB_REP, HEAD_DIM)
    qs = qs.transpose(0, 2, 3, 1, 4).reshape(nb, SB_KV_HEADS, Q_ROWS, HEAD_DIM)

    def new_page(a):
        return jnp.pad(a.reshape(nb, dec_seq * CHUNKS, LANES), ((0, 0), (0, (PAGE - dec_seq) * CHUNKS), (0, 0)))

    nvd = z32_s[:, C_VD:C_QS].reshape(nb, dec_seq, DIFF_KV_HEADS, 2, LANES).transpose(0, 1, 3, 2, 4)
    new_pages = (new_page(z32_s[:, C_KD:C_VD]), new_page(nvd),
                 new_page(z32_s[:, C_KS:C_VS]), new_page(z32_s[:, C_VS:C_GATE]))
    cvd = cache_v_diff[0].reshape(n_phys, PAGE, DIFF_KV_HEADS, 2, LANES).transpose(0, 1, 3, 2, 4)
    caches = (cache_k_diff[0].reshape(n_phys, PAGE * CHUNKS, LANES), cvd.reshape(n_phys, PAGE * CHUNKS, LANES),
              cache_k_sb[0].reshape(n_phys, PAGE * CHUNKS, LANES), cache_v_sb[0].reshape(n_phys, PAGE * CHUNKS, LANES))
    od_s, os_s = _decode(page_table, lams, subln_g, qd, qs, new_pages, caches)
    od_s = od_s.reshape(nb, DIFF_KV_HEADS, DIFF_REP, dec_seq, DIFF_V_DIM).transpose(0, 3, 1, 2, 4).reshape(ts, D_MODEL)
    os_s = os_s.reshape(nb, SB_KV_HEADS, SB_REP, dec_seq, HEAD_DIM).transpose(0, 3, 1, 2, 4).reshape(ts, D_MODEL)
    y_s = _post_attention(xs, od_s, os_s, z32_s, w, ts)

    def kv(z, rows, lead):
        return (z[:rows, C_KD:C_VD].reshape(lead + (DIFF_KV_HEADS, 2, HEAD_DIM)),
                z[:rows, C_VD:C_QS].reshape(lead + (DIFF_KV_HEADS, DIFF_V_DIM)),
                z[:rows, C_KS:C_VS].reshape(lead + (SB_KV_HEADS, HEAD_DIM)),
                z[:rows, C_VS:C_GATE].reshape(lead + (SB_KV_HEADS, HEAD_DIM)))

    y_prompt = y_p[N_META:tp_real].reshape(1, seq, D_MODEL)
    y_sample = y_s.reshape(nb, dec_seq, D_MODEL)
    return (y_prompt, y_sample) + kv(z32_p, tp_real, (1, 1, tp_real)) + kv(z32_s, ts, (1, nb, dec_seq))
```

```python
import functools
import math

import jax
import jax.numpy as jnp
import numpy as np
from jax import lax
from jax.experimental import pallas as pl
from jax.experimental.pallas import tpu as pltpu

F32 = jnp.float32
BF16 = jnp.bfloat16

D_MODEL = 2048
N_META = 16
HEAD_DIM = 128
DIFF_KV_HEADS = 2
DIFF_REP = 4
DIFF_V_DIM = 2 * HEAD_DIM
SB_KV_HEADS = 4
SB_REP = 4
ROPE_THETA = 10000.0
N_GROUPS = 4
EXPERTS_PER_GROUP = 8
N_EXPERTS = N_GROUPS * EXPERTS_PER_GROUP
D_EXPERT = 512
RMS_EPS = 1e-6
NEG_INF = -1e30
LAM_INIT = 0.8 - 0.6 * math.exp(-0.3 * 0)
SCALE = HEAD_DIM ** -0.5

C_QD, C_KD, C_VD, C_QS, C_KS, C_VS, C_GATE = 0, 2048, 2560, 3072, 5120, 5632, 6144
IN_COLS = 10240
ATTN_COLS = C_GATE

LANES = 128
PROJ_TN = 512
ROW_BLOCK = 768
ATT_BLOCK = 256
PAGES_PER_STEP = 8
VMEM_LIMIT = 56 * 1024 * 1024


def _cparams(sem):
    return pltpu.CompilerParams(dimension_semantics=sem, vmem_limit_bytes=VMEM_LIMIT)


def _dot(a, b):
    return jnp.dot(a, b, preferred_element_type=F32)


def _dot_nt(a, b):
    return lax.dot_general(a, b, (((1,), (1,)), ((), ())), preferred_element_type=F32)


def _rms(x, g):
    return x * lax.rsqrt(jnp.mean(x * x, axis=-1, keepdims=True) + RMS_EPS) * g


def _rope_table_kernel(pos_ref, inv_ref, cos_ref, sin_ref):
    ang = pos_ref[...] * inv_ref[...]
    lane = lax.broadcasted_iota(jnp.int32, ang.shape, 1)
    s = jnp.sin(ang)
    cos_ref[...] = jnp.cos(ang)
    sin_ref[...] = jnp.where(lane < HEAD_DIM // 2, -s, s)


def _rope_tables(pos, inv, tm):
    t = pos.shape[0]
    return pl.pallas_call(
        _rope_table_kernel,
        grid=(t // tm,),
        in_specs=[pl.BlockSpec((tm, 1), lambda i: (i, 0)), pl.BlockSpec((1, LANES), lambda i: (0, 0))],
        out_specs=[pl.BlockSpec((tm, LANES), lambda i: (i, 0))] * 2,
        out_shape=[jax.ShapeDtypeStruct((t, LANES), F32)] * 2,
        compiler_params=_cparams(("arbitrary",)),
        name="rope_tables",
    )(pos, inv)


N_ROPE_BLOCKS = C_VD // PROJ_TN
N_ATTN_BLOCKS = ATTN_COLS // PROJ_TN


def _proj_kernel(x_ref, g_ref, w_ref, cos_ref, sin_ref, z_ref, zb_ref, h_ref):
    j = pl.program_id(1)

    @pl.when(j == 0)
    def _():
        h_ref[...] = _rms(x_ref[...], g_ref[...]).astype(BF16)

    acc = _dot(h_ref[...], w_ref[...])

    @pl.when(j < N_ROPE_BLOCKS)
    def _():
        cos = cos_ref[...]
        sin = sin_ref[...]
        parts = []
        for c in range(PROJ_TN // HEAD_DIM):
            a = acc[:, c * HEAD_DIM:(c + 1) * HEAD_DIM]
            parts.append(a * cos + pltpu.roll(a, HEAD_DIM // 2, 1) * sin)
        r = jnp.concatenate(parts, axis=1)
        z_ref[...] = r
        zb_ref[...] = r.astype(BF16)

    @pl.when((j >= N_ROPE_BLOCKS) & (j < N_ATTN_BLOCKS))
    def _():
        z_ref[...] = acc
        zb_ref[...] = acc.astype(BF16)

    @pl.when(j >= N_ATTN_BLOCKS)
    def _():
        z_ref[...] = acc


def _project(x, g, w_bf16, cos, sin, tm):
    t = x.shape[0]
    return pl.pallas_call(
        _proj_kernel,
        grid=(t // tm, IN_COLS // PROJ_TN),
        in_specs=[
            pl.BlockSpec((tm, D_MODEL), lambda i, j: (i, 0)),
            pl.BlockSpec((1, D_MODEL), lambda i, j: (0, 0)),
            pl.BlockSpec((D_MODEL, PROJ_TN), lambda i, j: (0, j)),
            pl.BlockSpec((tm, LANES), lambda i, j: (i, 0)),
            pl.BlockSpec((tm, LANES), lambda i, j: (i, 0)),
        ],
        out_specs=[
            pl.BlockSpec((tm, PROJ_TN), lambda i, j: (i, j)),
            pl.BlockSpec((tm, PROJ_TN), lambda i, j: (i, jnp.minimum(j, N_ATTN_BLOCKS - 1))),
        ],
        out_shape=[jax.ShapeDtypeStruct((t, IN_COLS), F32), jax.ShapeDtypeStruct((t, ATTN_COLS), BF16)],
        scratch_shapes=[pltpu.VMEM((tm, D_MODEL), BF16)],
        compiler_params=_cparams(("arbitrary", "arbitrary")),
        name="in_proj",
    )(x, g, w_bf16, cos, sin)


def _lambda(lq1, lk1, lq2, lk2):
    s1 = jnp.sum(lq1[...] * lk1[...], axis=-1, keepdims=True)
    s2 = jnp.sum(lq2[...] * lk2[...], axis=-1, keepdims=True)
    return jnp.exp(s1) - jnp.exp(s2) + LAM_INIT


def _diff_finish(acc1, l1, acc2, l2, lam, subln):
    w = acc1.shape[1]
    o = acc1 / _lanes(l1, w) - lam * (acc2 / _lanes(l2, w))
    o = o * lax.rsqrt(jnp.mean(o * o, axis=-1, keepdims=True) + RMS_EPS)
    return o * subln * (1.0 - LAM_INIT)


def _lanes(x, width):
    return x if width == LANES else jnp.concatenate([x] * (width // LANES), axis=1)


def _softmax_step(s, v, m_ref, l_ref, acc_ref, idx):
    m_old = m_ref[idx]
    m_new = jnp.maximum(m_old, jnp.max(s, axis=-1, keepdims=True))
    alpha = jnp.exp(m_old - m_new)
    p = jnp.exp(s - _lanes(m_new, s.shape[1]))
    l_ref[idx] = alpha * l_ref[idx] + jnp.sum(p, axis=-1, keepdims=True)
    acc_ref[idx] = _lanes(alpha, v.shape[1]) * acc_ref[idx] + _dot(p.astype(BF16), v)
    m_ref[idx] = m_new


def _strict_lower(n):
    r = lax.broadcasted_iota(jnp.int32, (n, n), 0)
    c = lax.broadcasted_iota(jnp.int32, (n, n), 1)
    return jnp.where(r > c, 1.0, 0.0).astype(BF16)


def _log_keep(z):
    return -(jnp.maximum(z, 0.0) + jnp.log1p(jnp.exp(-jnp.abs(z))))


def _split_bf16(x):
    hi = x.astype(BF16)
    return hi, (x - hi.astype(F32)).astype(BF16)


SB_CUTOFF = -104.0


def _stick_step(z, v, u, keep, r_ref, acc_ref, idx):
    lk = _log_keep(z)
    lkm = lk if keep is None else jnp.where(keep, lk, 0.0)
    hi, lo = _split_bf16(lkm)
    suffix = _dot(hi, u) + _dot(lo, u)
    r = r_ref[idx]
    a = jnp.exp(z + lk + suffix + _lanes(r, z.shape[1]))
    if keep is not None:
        a = jnp.where(keep, a, 0.0)
    acc_ref[idx] = acc_ref[idx] + _dot(a.astype(BF16), v)
    r_ref[idx] = r + jnp.sum(lkm, axis=-1, keepdims=True)


def _diff_prompt_kernel(lq1, lk1, lq2, lk2, sg_ref, q_ref, k_ref, v_ref, o_ref, m_ref, l_ref, acc_ref):
    qi = pl.program_id(1)
    tb = ATT_BLOCK
    m_ref[...] = jnp.full(m_ref.shape, NEG_INF, F32)
    l_ref[...] = jnp.zeros(l_ref.shape, F32)
    acc_ref[...] = jnp.zeros(acc_ref.shape, F32)

    def block(kb, causal):
        start = pl.multiple_of(kb * tb, tb)
        kblk = k_ref[pl.ds(start, tb), :]
        vblk = v_ref[pl.ds(start, tb), :]
        for r in range(DIFF_REP):
            for m in range(2):
                c0 = (r * 2 + m) * HEAD_DIM
                s = _dot_nt(q_ref[:, c0:c0 + HEAD_DIM], kblk[:, m * HEAD_DIM:(m + 1) * HEAD_DIM]) * SCALE
                if causal is not None:
                    s = jnp.where(causal, s, NEG_INF)
                _softmax_step(s, vblk, m_ref, l_ref, acc_ref, r * 2 + m)

    def body(kb, carry):
        block(kb, None)
        return carry

    lax.fori_loop(0, qi, body, 0)
    row = lax.broadcasted_iota(jnp.int32, (tb, tb), 0)
    col = lax.broadcasted_iota(jnp.int32, (tb, tb), 1)
    block(qi, col <= row)

    lam = _lambda(lq1, lk1, lq2, lk2)
    for r in range(DIFF_REP):
        o = _diff_finish(acc_ref[2 * r], l_ref[2 * r], acc_ref[2 * r + 1], l_ref[2 * r + 1], lam, sg_ref[...])
        o_ref[:, r * DIFF_V_DIM:(r + 1) * DIFF_V_DIM] = o.astype(BF16)


def _lam_specs(nd):
    zero = (lambda *a: (0, 0))
    return [pl.BlockSpec((1, HEAD_DIM), zero)] * 4 + [pl.BlockSpec((1, DIFF_V_DIM), zero)]


def _diff_prompt(zb, lams, subln):
    t = zb.shape[0]
    tb = ATT_BLOCK
    qw = DIFF_REP * 2 * HEAD_DIM
    return pl.pallas_call(
        _diff_prompt_kernel,
        grid=(DIFF_KV_HEADS, t // tb),
        in_specs=_lam_specs(2) + [
            pl.BlockSpec((tb, qw), lambda g, i: (i, g)),
            pl.BlockSpec((t, 2 * HEAD_DIM), lambda g, i: (0, C_KD // (2 * HEAD_DIM) + g)),
            pl.BlockSpec((t, DIFF_V_DIM), lambda g, i: (0, C_VD // DIFF_V_DIM + g)),
        ],
        out_specs=pl.BlockSpec((tb, DIFF_REP * DIFF_V_DIM), lambda g, i: (i, g)),
        out_shape=jax.ShapeDtypeStruct((t, D_MODEL), BF16),
        scratch_shapes=[
            pltpu.VMEM((2 * DIFF_REP, tb, LANES), F32),
            pltpu.VMEM((2 * DIFF_REP, tb, LANES), F32),
            pltpu.VMEM((2 * DIFF_REP, tb, DIFF_V_DIM), F32),
        ],
        compiler_params=_cparams(("arbitrary", "arbitrary")),
        name="diff_prompt",
    )(*lams, subln, zb, zb, zb)


def _sb_prompt_kernel(q_ref, k_ref, v_ref, o_ref, r_ref, acc_ref):
    qi = pl.program_id(1)
    tb = ATT_BLOCK
    r_ref[...] = jnp.zeros(r_ref.shape, F32)
    acc_ref[...] = jnp.zeros(acc_ref.shape, F32)
    u = _strict_lower(tb)

    def block(kb, keep):
        start = pl.multiple_of(kb * tb, tb)
        kblk = k_ref[pl.ds(start, tb), :]
        vblk = v_ref[pl.ds(start, tb), :]
        for r in range(SB_REP):
            z = _dot_nt(q_ref[:, r * HEAD_DIM:(r + 1) * HEAD_DIM], kblk) * SCALE
            _stick_step(z, vblk, u, keep, r_ref, acc_ref, r)

    row = lax.broadcasted_iota(jnp.int32, (tb, tb), 0)
    col = lax.broadcasted_iota(jnp.int32, (tb, tb), 1)
    block(qi, col < row)

    def live(c):
        return (c[0] >= 0) & (c[1] > SB_CUTOFF)

    def body(c):
        block(c[0], None)
        return c[0] - 1, jnp.max(r_ref[...])

    lax.while_loop(live, body, (qi - 1, jnp.max(r_ref[...])))
    for r in range(SB_REP):
        o_ref[:, r * HEAD_DIM:(r + 1) * HEAD_DIM] = acc_ref[r].astype(BF16)


def _sb_prompt(zb):
    t = zb.shape[0]
    tb = ATT_BLOCK
    qw = SB_REP * HEAD_DIM
    return pl.pallas_call(
        _sb_prompt_kernel,
        grid=(SB_KV_HEADS, t // tb),
        in_specs=[
            pl.BlockSpec((tb, qw), lambda h, i: (i, C_QS // qw + h)),
            pl.BlockSpec((t, HEAD_DIM), lambda h, i: (0, C_KS // HEAD_DIM + h)),
            pl.BlockSpec((t, HEAD_DIM), lambda h, i: (0, C_VS // HEAD_DIM + h)),
        ],
        out_specs=pl.BlockSpec((tb, qw), lambda h, i: (i, h)),
        out_shape=jax.ShapeDtypeStruct((t, D_MODEL), BF16),
        scratch_shapes=[pltpu.VMEM((SB_REP, tb, LANES), F32), pltpu.VMEM((SB_REP, tb, HEAD_DIM), F32)],
        compiler_params=_cparams(("arbitrary", "arbitrary")),
        name="sb_prompt",
    )(zb, zb, zb)


PAGE = 128
CHUNKS = 4
Q_ROWS = 16
DEC_SEQ = 4


def _page_chunk(ref, c):
    return ref[pl.ds(c, PAGE, stride=CHUNKS), :].astype(BF16)


def _decode_diff(qd_ref, kds, vds, causal, m_ref, l_ref, acc_ref):
    n = len(kds)
    for g in range(DIFF_KV_HEADS):
        v = [[_page_chunk(vd, c * 2 + g) for c in range(2)] for vd in vds]
        for m in range(2):
            idx = g * 2 + m
            q = qd_ref[idx]
            s = [_dot_nt(q, _page_chunk(kd, idx)) * SCALE for kd in kds]
            if causal is not None:
                s = [jnp.where(causal, x, NEG_INF) for x in s]
            m_old = m_ref[idx]
            m_new = jnp.maximum(m_old, jnp.max(functools.reduce(jnp.maximum, s), axis=-1, keepdims=True))
            alpha = jnp.exp(m_old - m_new)
            p = [jnp.exp(x - m_new) for x in s]
            l_ref[idx] = alpha * l_ref[idx] + jnp.sum(functools.reduce(jnp.add, p), axis=-1, keepdims=True)
            pb = [x.astype(BF16) for x in p]
            pv = [functools.reduce(jnp.add, [_dot(pb[i], v[i][c]) for i in range(n)]) for c in range(2)]
            acc_ref[idx] = _lanes(alpha, DIFF_V_DIM) * acc_ref[idx] + jnp.concatenate(pv, axis=1)
            m_ref[idx] = m_new


def _decode_stick(qs_ref, kss, vss, strict, u, r_ref, acc_ref):
    n = len(kss)
    rows = n * Q_ROWS
    for h in range(SB_KV_HEADS):
        q = qs_ref[h]
        z = jnp.concatenate([_dot_nt(q, _page_chunk(ks, h)) for ks in kss], axis=0) * SCALE
        lk = _log_keep(z)
        keep = None if strict is None else jnp.concatenate([strict] * n, axis=0)
        lkm = lk if keep is None else jnp.where(keep, lk, 0.0)
        hi, lo = _split_bf16(lkm)
        s2 = _dot(jnp.concatenate([hi, lo], axis=0), u)
        suffix = s2[:rows] + s2[rows:]
        tot = jnp.sum(lkm, axis=-1, keepdims=True)
        r = r_ref[h]
        offs = []
        for i in range(n):
            offs.append(r)
            r = r + tot[i * Q_ROWS:(i + 1) * Q_ROWS]
        a = jnp.exp(z + lk + suffix + jnp.concatenate(offs, axis=0))
        if keep is not None:
            a = jnp.where(keep, a, 0.0)
        ab = a.astype(BF16)
        pv = [_dot(ab[i * Q_ROWS:(i + 1) * Q_ROWS], _page_chunk(vss[i], h)) for i in range(n)]
        acc_ref[h] = acc_ref[h] + functools.reduce(jnp.add, pv)
        r_ref[h] = r


def _decode_kernel(pt_ref, lq1, lk1, lq2, lk2, sg_ref, qd_ref, qs_ref, nkd, nvd, nks, nvs, *rest):
    n_in = 4 * PAGES_PER_STEP
    pages = rest[:n_in]
    od_ref, os_ref = rest[n_in:n_in + 2]
    m_ref, l_ref, accd_ref, r_ref, accs_ref = rest[n_in + 2:]
    s_id = pl.program_id(1)
    u = _strict_lower(PAGE)

    @pl.when(s_id == 0)
    def _():
        m_ref[...] = jnp.full(m_ref.shape, NEG_INF, F32)
        l_ref[...] = jnp.zeros(l_ref.shape, F32)
        accd_ref[...] = jnp.zeros(accd_ref.shape, F32)
        r_ref[...] = jnp.zeros(r_ref.shape, F32)
        accs_ref[...] = jnp.zeros(accs_ref.shape, F32)
        tq = lax.broadcasted_iota(jnp.int32, (Q_ROWS, PAGE), 0) & (DEC_SEQ - 1)
        key = lax.broadcasted_iota(jnp.int32, (Q_ROWS, PAGE), 1)
        _decode_diff(qd_ref, [nkd], [nvd], key <= tq, m_ref, l_ref, accd_ref)
        _decode_stick(qs_ref, [nks], [nvs], key < tq, u, r_ref, accs_ref)

    _decode_diff(qd_ref, pages[0::4], pages[1::4], None, m_ref, l_ref, accd_ref)

    @pl.when(jnp.max(r_ref[...]) > SB_CUTOFF)
    def _():
        _decode_stick(qs_ref, pages[2::4], pages[3::4], None, u, r_ref, accs_ref)

    @pl.when(s_id == pl.num_programs(1) - 1)
    def _():
        lam = _lambda(lq1, lk1, lq2, lk2)
        for g in range(DIFF_KV_HEADS):
            o = _diff_finish(accd_ref[2 * g], l_ref[2 * g], accd_ref[2 * g + 1], l_ref[2 * g + 1], lam, sg_ref[...])
            od_ref[g] = o.astype(BF16)
        for h in range(SB_KV_HEADS):
            os_ref[h] = accs_ref[h].astype(BF16)


def _decode(page_table, lams, subln, qd, qs, new_pages, caches):
    nb, n_pages = page_table.shape
    steps = n_pages // PAGES_PER_STEP
    zero2 = lambda b, s, pt: (0, 0)
    per_b4 = lambda b, s, pt: (b, 0, 0, 0)
    per_b3 = lambda b, s, pt: (b, 0, 0)

    def page_spec(p):
        return pl.BlockSpec((None, PAGE * CHUNKS, LANES),
                            lambda b, s, pt: (pt[b, n_pages - 1 - (s * PAGES_PER_STEP + p)], 0, 0))

    in_specs = ([pl.BlockSpec((1, HEAD_DIM), zero2)] * 4 + [pl.BlockSpec((1, DIFF_V_DIM), zero2)]
                + [pl.BlockSpec((None, 4, Q_ROWS, HEAD_DIM), per_b4)] * 2
                + [pl.BlockSpec((None, PAGE * CHUNKS, LANES), per_b3)] * 4)
    page_args = []
    for p in range(PAGES_PER_STEP):
        in_specs += [page_spec(p)] * 4
        page_args += list(caches)
    grid_spec = pltpu.PrefetchScalarGridSpec(
        num_scalar_prefetch=1,
        grid=(nb, steps),
        in_specs=in_specs,
        out_specs=[pl.BlockSpec((None, DIFF_KV_HEADS, Q_ROWS, DIFF_V_DIM), per_b4),
                   pl.BlockSpec((None, SB_KV_HEADS, Q_ROWS, HEAD_DIM), per_b4)],
        scratch_shapes=[
            pltpu.VMEM((4, Q_ROWS, LANES), F32), pltpu.VMEM((4, Q_ROWS, LANES), F32),
            pltpu.VMEM((4, Q_ROWS, DIFF_V_DIM), F32),
            pltpu.VMEM((SB_KV_HEADS, Q_ROWS, LANES), F32), pltpu.VMEM((SB_KV_HEADS, Q_ROWS, HEAD_DIM), F32),
        ],
    )
    return pl.pallas_call(
        _decode_kernel,
        grid_spec=grid_spec,
        out_shape=[jax.ShapeDtypeStruct((nb, DIFF_KV_HEADS, Q_ROWS, DIFF_V_DIM), BF16),
                   jax.ShapeDtypeStruct((nb, SB_KV_HEADS, Q_ROWS, HEAD_DIM), BF16)],
        compiler_params=_cparams(("arbitrary", "arbitrary")),
        name="decode_attn",
    )(page_table, *lams, subln, qd, qs, *new_pages, *page_args)


def _merge_kernel(od_ref, os_ref, gd_ref, gs_ref, wd_ref, ws_ref, o_ref):
    g_d = 1.0 / (1.0 + jnp.exp(-gd_ref[...]))
    g_s = 1.0 / (1.0 + jnp.exp(-gs_ref[...]))
    o_ref[...] = (g_d * _dot(od_ref[...], wd_ref[...]) + g_s * _dot(os_ref[...], ws_ref[...])).astype(BF16)


def _merge(od, os_, z32, wbd, wbs, tm):
    t = od.shape[0]
    tn = PROJ_TN
    gd0 = C_GATE // tn
    gs0 = (C_GATE + D_MODEL) // tn
    return pl.pallas_call(
        _merge_kernel,
        grid=(t // tm, D_MODEL // tn),
        in_specs=[
            pl.BlockSpec((tm, D_MODEL), lambda i, j: (i, 0)),
            pl.BlockSpec((tm, D_MODEL), lambda i, j: (i, 0)),
            pl.BlockSpec((tm, tn), lambda i, j: (i, gd0 + j)),
            pl.BlockSpec((tm, tn), lambda i, j: (i, gs0 + j)),
            pl.BlockSpec((D_MODEL, tn), lambda i, j: (0, j)),
            pl.BlockSpec((D_MODEL, tn), lambda i, j: (0, j)),
        ],
        out_specs=pl.BlockSpec((tm, tn), lambda i, j: (i, j)),
        out_shape=jax.ShapeDtypeStruct((t, D_MODEL), BF16),
        compiler_params=_cparams(("arbitrary", "arbitrary")),
        name="merge",
    )(od, os_, z32, z32, wbd, wbs)


def _out_kernel(x_ref, m_ref, w_ref, o_ref):
    o_ref[...] = x_ref[...] + _dot(m_ref[...], w_ref[...])


def _out_proj(x, merged, w_out, tm):
    t = x.shape[0]
    tn = PROJ_TN
    return pl.pallas_call(
        _out_kernel,
        grid=(t // tm, D_MODEL // tn),
        in_specs=[
            pl.BlockSpec((tm, tn), lambda i, j: (i, j)),
            pl.BlockSpec((tm, D_MODEL), lambda i, j: (i, 0)),
            pl.BlockSpec((D_MODEL, tn), lambda i, j: (0, j)),
        ],
        out_specs=pl.BlockSpec((tm, tn), lambda i, j: (i, j)),
        out_shape=jax.ShapeDtypeStruct((t, D_MODEL), F32),
        compiler_params=_cparams(("arbitrary", "arbitrary")),
        name="out_proj",
    )(x, merged, w_out)


def _router_kernel(x_ref, g_ref, w_ref, b_ref, hn_ref, comb_ref, route_ref, counts_ref, carry_ref):
    hn32 = _rms(x_ref[...], g_ref[...])
    hn = hn32.astype(BF16)
    hn_ref[...] = hn32.astype(hn_ref.dtype)
    logits = _dot(hn, w_ref[...]) + b_ref[...]
    gl = logits[:, :LANES]
    el = logits[:, LANES:]
    lane = lax.broadcasted_iota(jnp.int32, gl.shape, 1).astype(F32)
    far = float(LANES)

    def first_max(v):
        mx = jnp.max(v, axis=-1, keepdims=True)
        return mx, jnp.min(jnp.where(v == mx, lane, far), axis=-1, keepdims=True)

    gl = jnp.where(lane < N_GROUPS, gl, NEG_INF)
    gmax, grp = first_max(gl)
    p_grp = 1.0 / jnp.sum(jnp.exp(gl - gmax), axis=-1, keepdims=True)
    lo = grp * EXPERTS_PER_GROUP
    es = jnp.where((lane >= lo) & (lane < lo + EXPERTS_PER_GROUP), el, NEG_INF)
    v1, i1 = first_max(es)
    es2 = jnp.where(lane == i1, NEG_INF, es)
    v2, i2 = first_max(es2)
    e2 = jnp.exp(v2 - v1)
    w1 = 1.0 / (1.0 + e2)
    w2 = e2 / (1.0 + e2)
    ga = p_grp * w1
    gb = p_grp * w2
    oh_a = jnp.where(lane == i1, 1.0, 0.0)
    oh_b = jnp.where(lane == i2, 1.0, 0.0)
    comb_ref[...] = ga * oh_a + gb * oh_b

    @pl.when(pl.program_id(0) == 0)
    def _():
        carry_ref[...] = jnp.zeros(carry_ref.shape, F32)

    tm = x_ref.shape[0]
    oh = oh_a + oh_b
    before = _dot(_strict_lower(tm), oh.astype(BF16)) + carry_ref[...]
    rank_a = jnp.sum(before * oh_a, axis=-1, keepdims=True)
    rank_b = jnp.sum(before * oh_b, axis=-1, keepdims=True)
    carry_ref[...] = carry_ref[...] + jnp.sum(oh, axis=0, keepdims=True)
    counts_ref[...] = carry_ref[...]
    fields = (i1, i2, ga, gb, rank_a, rank_b)
    route_ref[...] = functools.reduce(jnp.add, [jnp.where(lane == float(k), f, 0.0) for k, f in enumerate(fields)])


def _router(x2, g, w_r, b_r, tm, hn_dtype):
    t = x2.shape[0]
    return pl.pallas_call(
        _router_kernel,
        grid=(t // tm,),
        in_specs=[
            pl.BlockSpec((tm, D_MODEL), lambda i: (i, 0)),
            pl.BlockSpec((1, D_MODEL), lambda i: (0, 0)),
            pl.BlockSpec((D_MODEL, 2 * LANES), lambda i: (0, 0)),
            pl.BlockSpec((1, 2 * LANES), lambda i: (0, 0)),
        ],
        out_specs=[pl.BlockSpec((tm, D_MODEL), lambda i: (i, 0)), pl.BlockSpec((tm, LANES), lambda i: (i, 0)),
                   pl.BlockSpec((tm, LANES), lambda i: (i, 0)), pl.BlockSpec((1, LANES), lambda i: (0, 0))],
        out_shape=[jax.ShapeDtypeStruct((t, D_MODEL), hn_dtype), jax.ShapeDtypeStruct((t, LANES), F32),
                   jax.ShapeDtypeStruct((t, LANES), F32), jax.ShapeDtypeStruct((1, LANES), F32)],
        scratch_shapes=[pltpu.VMEM((1, LANES), F32)],
        compiler_params=_cparams(("arbitrary",)),
        name="router",
    )(x2, g, w_r, b_r)


MOE_TM = 256


def _moe_routed_kernel(te_ref, src_ref, dst_ref, gate_ref, wg_ref, wu_ref, wd_ref, hn_hbm, y_hbm,
                       xbuf, ybuf, gsem, ssem):
    j = pl.program_id(0)

    def row_in(r):
        return pltpu.make_async_copy(hn_hbm.at[pl.ds(src_ref[0, r], 1), :], xbuf.at[pl.ds(r, 1), :], gsem)

    def row_out(r):
        return pltpu.make_async_copy(ybuf.at[pl.ds(r, 1), :], y_hbm.at[pl.ds(dst_ref[0, r], 1), :], ssem)

    def each_row(fn, only_real):
        def body(r, c):
            if only_real:
                pl.when(dst_ref[0, r] >= 0)(lambda: fn(r))
            else:
                fn(r)
            return c
        lax.fori_loop(0, MOE_TM, body, 0)

    @pl.when(te_ref[j] < N_EXPERTS)
    def _():
        each_row(lambda r: row_in(r).start(), False)
        each_row(lambda r: row_in(r).wait(), False)
        x = xbuf[...].astype(BF16)
        gate = _dot(x, wg_ref[...])
        hid = gate * (1.0 / (1.0 + jnp.exp(-gate))) * _dot(x, wu_ref[...])
        ybuf[...] = _dot((hid * gate_ref[...]).astype(BF16), wd_ref[...])
        each_row(lambda r: row_out(r).start(), True)
        each_row(lambda r: row_out(r).wait(), True)


def _moe_routed(hn, route, counts, wg, wu, wd):
    n = hn.shape[0]
    n_tiles = (2 * n) // MOE_TM + N_EXPERTS
    p = n_tiles * MOE_TM
    experts = route[:, 0:2].astype(jnp.int32)
    gates = route[:, 2:4]
    ranks = route[:, 4:6].astype(jnp.int32)
    cnt = counts[0, :N_EXPERTS].astype(jnp.int32)
    padded = (cnt + MOE_TM - 1) // MOE_TM * MOE_TM
    ends = jnp.cumsum(padded)
    slot = ((ends - padded)[experts] + ranks).reshape(-1)
    tok = jnp.arange(n, dtype=jnp.int32)
    src = jnp.zeros((p,), jnp.int32).at[slot].set(jnp.stack([tok, tok], axis=1).reshape(-1))
    dst = jnp.full((p,), -1, jnp.int32).at[slot].set(jnp.stack([tok, tok + n], axis=1).reshape(-1))
    gate_slot = jnp.zeros((p,), F32).at[slot].set(gates.reshape(-1)).reshape(p, 1)
    tile_expert = jnp.sum(jnp.arange(n_tiles, dtype=jnp.int32)[:, None] * MOE_TM >= ends[None, :], axis=1)
    tile_expert = tile_expert.astype(jnp.int32)

    def w_spec(shape):
        return pl.BlockSpec((None,) + shape, lambda j, te: (jnp.minimum(te[j], N_EXPERTS - 1), 0, 0))

    idx_spec = pl.BlockSpec((None, 1, MOE_TM), lambda j, te: (j, 0, 0), memory_space=pltpu.SMEM)
    grid_spec = pltpu.PrefetchScalarGridSpec(
        num_scalar_prefetch=1,
        grid=(n_tiles,),
        in_specs=[idx_spec, idx_spec, pl.BlockSpec((MOE_TM, 1), lambda j, te: (j, 0)),
                  w_spec((D_MODEL, D_EXPERT)), w_spec((D_MODEL, D_EXPERT)), w_spec((D_EXPERT, D_MODEL)),
                  pl.BlockSpec(memory_space=pl.ANY)],
        out_specs=pl.BlockSpec(memory_space=pl.ANY),
        scratch_shapes=[pltpu.VMEM((MOE_TM, D_MODEL), F32), pltpu.VMEM((MOE_TM, D_MODEL), F32),
                        pltpu.SemaphoreType.DMA(()), pltpu.SemaphoreType.DMA(())],
    )
    return pl.pallas_call(
        _moe_routed_kernel,
        grid_spec=grid_spec,
        out_shape=jax.ShapeDtypeStruct((2 * n, D_MODEL), F32),
        compiler_params=_cparams(("arbitrary",)),
        name="moe_routed",
    )(tile_expert, src.reshape(n_tiles, 1, MOE_TM), dst.reshape(n_tiles, 1, MOE_TM), gate_slot, wg, wu, wd, hn)


def _combine_kernel(x_ref, ya_ref, yb_ref, g_ref, o_ref):
    o_ref[...] = _rms(x_ref[...] + (ya_ref[...] + yb_ref[...]), g_ref[...])


def _combine(x2, y2, gf, tm):
    t = x2.shape[0]
    nblk = t // tm
    return pl.pallas_call(
        _combine_kernel,
        grid=(nblk,),
        in_specs=[pl.BlockSpec((tm, D_MODEL), lambda i: (i, 0)),
                  pl.BlockSpec((tm, D_MODEL), lambda i: (i, 0)),
                  pl.BlockSpec((tm, D_MODEL), lambda i: (i + nblk, 0)),
                  pl.BlockSpec((1, D_MODEL), lambda i: (0, 0))],
        out_specs=pl.BlockSpec((tm, D_MODEL), lambda i: (i, 0)),
        out_shape=jax.ShapeDtypeStruct((t, D_MODEL), F32),
        compiler_params=_cparams(("arbitrary",)),
        name="moe_combine",
    )(x2, y2, y2, gf)


def _moe_kernel(hn_ref, comb_ref, x_ref, wg_ref, wu_ref, wd_ref, gf_ref, y_ref):
    e = pl.program_id(1)

    @pl.when(e == 0)
    def _():
        y_ref[...] = x_ref[...]

    hn = hn_ref[...]
    lane = lax.broadcasted_iota(jnp.int32, comb_ref.shape, 1)
    c = jnp.sum(jnp.where(lane == e, comb_ref[...], 0.0), axis=-1, keepdims=True)
    gate = _dot(hn, wg_ref[...])
    hid = gate * (1.0 / (1.0 + jnp.exp(-gate))) * _dot(hn, wu_ref[...])
    y_ref[...] += _dot((hid * c).astype(BF16), wd_ref[...])

    @pl.when(e == pl.num_programs(1) - 1)
    def _():
        y_ref[...] = _rms(y_ref[...], gf_ref[...])


def _moe(hn, comb, x2, wg, wu, wd, gf, tm):
    t = hn.shape[0]
    return pl.pallas_call(
        _moe_kernel,
        grid=(t // tm, N_EXPERTS),
        in_specs=[
            pl.BlockSpec((tm, D_MODEL), lambda i, e: (i, 0)),
            pl.BlockSpec((tm, LANES), lambda i, e: (i, 0)),
            pl.BlockSpec((tm, D_MODEL), lambda i, e: (i, 0)),
            pl.BlockSpec((None, D_MODEL, D_EXPERT), lambda i, e: (e, 0, 0)),
            pl.BlockSpec((None, D_MODEL, D_EXPERT), lambda i, e: (e, 0, 0)),
            pl.BlockSpec((None, D_EXPERT, D_MODEL), lambda i, e: (e, 0, 0)),
            pl.BlockSpec((1, D_MODEL), lambda i, e: (0, 0)),
        ],
        out_specs=pl.BlockSpec((tm, D_MODEL), lambda i, e: (i, 0)),
        out_shape=jax.ShapeDtypeStruct((t, D_MODEL), F32),
        compiler_params=_cparams(("arbitrary", "arbitrary")),
        name="moe",
    )(hn, comb, x2, wg, wu, wd, gf)


def _post_attention(x, od, os_, z32, w, tm):
    merged = _merge(od, os_, z32, w["wbd"], w["wbs"], tm)
    x2 = _out_proj(x, merged, w["wout"], tm)
    if x.shape[0] // MOE_TM < N_EXPERTS:
        hn, comb, _, _ = _router(x2, w["g_ffn"], w["w_r"], w["b_r"], tm, BF16)
        return _moe(hn, comb, x2, w["wg"], w["wu"], w["wd"], w["g_final"], tm)
    hn, _, route, counts = _router(x2, w["g_ffn"], w["w_r"], w["b_r"], tm, F32)
    y2 = _moe_routed(hn, route, counts, w["wg"], w["wu"], w["wd"])
    return _combine(x2, y2, w["g_final"], tm)


def _pad_rows(a, rows):
    return jnp.pad(a, ((0, rows - a.shape[0]),) + ((0, 0),) * (a.ndim - 1))


def kernel(x_prompt, x_sample, cache_k_diff, cache_v_diff, cache_k_sb, cache_v_sb, page_table, meta_tokens, norm_mix_g, w_in, lambda_q1, lambda_k1, lambda_q2, lambda_k2, subln_g, w_branch_diff, w_branch_sb, w_out, norm_ffn_g, w_router_group, b_router_group, w_router_expert, b_router_expert, w_expert_gate, w_expert_up, w_expert_down, norm_final_g):
    assert x_prompt.shape[0] == 1 and norm_mix_g.shape[0] == 1
    seq = x_prompt.shape[1]
    tp_real = N_META + seq
    tp = -(-tp_real // ROW_BLOCK) * ROW_BLOCK
    assert tp % ATT_BLOCK == 0
    nb, dec_seq, _ = x_sample.shape
    ts = nb * dec_seq
    n_phys = cache_k_diff.shape[1]
    past_len = page_table.shape[1] * PAGE
    assert dec_seq == DEC_SEQ and cache_k_diff.shape[2] == PAGE

    w_in_b = w_in[0].astype(BF16)
    w_r = jnp.zeros((D_MODEL, 2 * LANES), F32)
    w_r = w_r.at[:, :N_GROUPS].set(w_router_group[0]).at[:, LANES:LANES + N_EXPERTS].set(w_router_expert[0])
    b_r = jnp.zeros((1, 2 * LANES), F32)
    b_r = b_r.at[0, :N_GROUPS].set(b_router_group[0]).at[0, LANES:LANES + N_EXPERTS].set(b_router_expert[0])
    w = dict(
        wbd=w_branch_diff[0].astype(BF16), wbs=w_branch_sb[0].astype(BF16), wout=w_out[0].astype(BF16),
        g_ffn=norm_ffn_g, w_r=w_r.astype(BF16), b_r=b_r,
        wg=w_expert_gate[0].astype(BF16), wu=w_expert_up[0].astype(BF16), wd=w_expert_down[0].astype(BF16),
        g_final=norm_final_g.reshape(1, D_MODEL),
    )
    lams = (lambda_q1, lambda_k1, lambda_q2, lambda_k2)

    half = HEAD_DIM // 2
    inv = ROPE_THETA ** (-jnp.arange(half, dtype=F32) * 2.0 / HEAD_DIM)
    inv = jnp.concatenate([inv, inv]).reshape(1, LANES)

    xp = _pad_rows(jnp.concatenate([meta_tokens.astype(F32), x_prompt[0]], axis=0), tp)
    pos_p = jnp.arange(tp, dtype=jnp.int32).astype(F32).reshape(tp, 1)
    cos_p, sin_p = _rope_tables(pos_p, inv, ROW_BLOCK)
    z32_p, zb_p = _project(xp, norm_mix_g, w_in_b, cos_p, sin_p, ROW_BLOCK)
    od_p = _diff_prompt(zb_p, lams, subln_g)
    os_p = _sb_prompt(zb_p)
    y_p = _post_attention(xp, od_p, os_p, z32_p, w, ROW_BLOCK)

    xs = x_sample.reshape(ts, D_MODEL)
    pos_s = jnp.tile(past_len + jnp.arange(dec_seq, dtype=jnp.int32), nb).astype(F32).reshape(ts, 1)
    cos_s, sin_s = _rope_tables(pos_s, inv, ts)
    z32_s, zb_s = _project(xs, norm_mix_g, w_in_b, cos_s, sin_s, ts)

    qd = zb_s[:, C_QD:C_KD].reshape(nb, dec_seq, DIFF_KV_HEADS, DIFF_REP, 2, HEAD_DIM)
    qd = qd.transpose(0, 2, 4, 3, 1, 5).reshape(nb, 4, Q_ROWS, HEAD_DIM)
    qs = zb_s[:, C_QS:C_KS].reshape(nb, dec_seq, SB_KV_HEADS, SB_REP, HEAD_DIM)
    qs = qs.transpose(0, 2, 3, 1, 4).reshape(nb, SB_KV_HEADS, Q_ROWS, HEAD_DIM)

    def new_page(a):
        return jnp.pad(a.reshape(nb, dec_seq * CHUNKS, LANES), ((0, 0), (0, (PAGE - dec_seq) * CHUNKS), (0, 0)))

    nvd = z32_s[:, C_VD:C_QS].reshape(nb, dec_seq, DIFF_KV_HEADS, 2, LANES).transpose(0, 1, 3, 2, 4)
    new_pages = (new_page(z32_s[:, C_KD:C_VD]), new_page(nvd),
                 new_page(z32_s[:, C_KS:C_VS]), new_page(z32_s[:, C_VS:C_GATE]))
    cvd = cache_v_diff[0].reshape(n_phys, PAGE, DIFF_KV_HEADS, 2, LANES).transpose(0, 1, 3, 2, 4)
    caches = (cache_k_diff[0].reshape(n_phys, PAGE * CHUNKS, LANES), cvd.reshape(n_phys, PAGE * CHUNKS, LANES),
              cache_k_sb[0].reshape(n_phys, PAGE * CHUNKS, LANES), cache_v_sb[0].reshape(n_phys, PAGE * CHUNKS, LANES))
    od_s, os_s = _decode(page_table, lams, subln_g, qd, qs, new_pages, caches)
    od_s = od_s.reshape(nb, DIFF_KV_HEADS, DIFF_REP, dec_seq, DIFF_V_DIM).transpose(0, 3, 1, 2, 4).reshape(ts, D_MODEL)
    os_s = os_s.reshape(nb, SB_KV_HEADS, SB_REP, dec_seq, HEAD_DIM).transpose(0, 3, 1, 2, 4).reshape(ts, D_MODEL)
    y_s = _post_attention(xs, od_s, os_s, z32_s, w, ts)

    def kv(z, rows, lead):
        return (z[:rows, C_KD:C_VD].reshape(lead + (DIFF_KV_HEADS, 2, HEAD_DIM)),
                z[:rows, C_VD:C_QS].reshape(lead + (DIFF_KV_HEADS, DIFF_V_DIM)),
                z[:rows, C_KS:C_VS].reshape(lead + (SB_KV_HEADS, HEAD_DIM)),
                z[:rows, C_VS:C_GATE].reshape(lead + (SB_KV_HEADS, HEAD_DIM)))

    y_prompt = y_p[N_META:tp_real].reshape(1, seq, D_MODEL)
    y_sample = y_s.reshape(nb, dec_seq, D_MODEL)
    return (y_prompt, y_sample) + kv(z32_p, tp_real, (1, 1, tp_real)) + kv(z32_s, ts, (1, nb, dec_seq))
```

```python
import functools
import math

import jax
import jax.numpy as jnp
import numpy as np
from jax import lax
from jax.experimental import pallas as pl
from jax.experimental.pallas import tpu as pltpu

F32 = jnp.float32
BF16 = jnp.bfloat16

D_MODEL = 2048
N_META = 16
HEAD_DIM = 128
DIFF_KV_HEADS = 2
DIFF_REP = 4
DIFF_V_DIM = 2 * HEAD_DIM
SB_KV_HEADS = 4
SB_REP = 4
ROPE_THETA = 10000.0
N_GROUPS = 4
EXPERTS_PER_GROUP = 8
N_EXPERTS = N_GROUPS * EXPERTS_PER_GROUP
D_EXPERT = 512
RMS_EPS = 1e-6
NEG_INF = -1e30
LAM_INIT = 0.8 - 0.6 * math.exp(-0.3 * 0)
SCALE = HEAD_DIM ** -0.5
LOG2E = math.log2(math.e)

C_QD, C_KD, C_VD, C_QS, C_KS, C_VS, C_GATE = 0, 2048, 2560, 3072, 5120, 5632, 6144
IN_COLS = 10240
ATTN_COLS = C_GATE

LANES = 128
PROJ_TN = 512
ROW_BLOCK = 768
ATT_BLOCK = 256
PAGES_PER_STEP = 8
VMEM_LIMIT = 56 * 1024 * 1024


def _cparams(sem):
    return pltpu.CompilerParams(dimension_semantics=sem, vmem_limit_bytes=VMEM_LIMIT)


def _dot(a, b):
    return jnp.dot(a, b, preferred_element_type=F32)


def _dot_nt(a, b):
    return lax.dot_general(a, b, (((1,), (1,)), ((), ())), preferred_element_type=F32)


def _rms(x, g):
    return x * lax.rsqrt(jnp.mean(x * x, axis=-1, keepdims=True) + RMS_EPS) * g


def _rope_table_kernel(pos_ref, inv_ref, cos_ref, sin_ref):
    ang = pos_ref[...] * inv_ref[...]
    lane = lax.broadcasted_iota(jnp.int32, ang.shape, 1)
    s = jnp.sin(ang)
    cos_ref[...] = jnp.cos(ang)
    sin_ref[...] = jnp.where(lane < HEAD_DIM // 2, -s, s)


def _rope_tables(pos, inv, tm):
    t = pos.shape[0]
    return pl.pallas_call(
        _rope_table_kernel,
        grid=(t // tm,),
        in_specs=[pl.BlockSpec((tm, 1), lambda i: (i, 0)), pl.BlockSpec((1, LANES), lambda i: (0, 0))],
        out_specs=[pl.BlockSpec((tm, LANES), lambda i: (i, 0))] * 2,
        out_shape=[jax.ShapeDtypeStruct((t, LANES), F32)] * 2,
        compiler_params=_cparams(("arbitrary",)),
        name="rope_tables",
    )(pos, inv)


N_ROPE_BLOCKS = C_VD // PROJ_TN
N_ATTN_BLOCKS = ATTN_COLS // PROJ_TN


def _proj_kernel(x_ref, g_ref, w_ref, cos_ref, sin_ref, z_ref, zb_ref, h_ref):
    j = pl.program_id(1)

    @pl.when(j == 0)
    def _():
        h_ref[...] = _rms(x_ref[...], g_ref[...]).astype(BF16)

    acc = _dot(h_ref[...], w_ref[...])

    @pl.when(j < N_ROPE_BLOCKS)
    def _():
        cos = cos_ref[...]
        sin = sin_ref[...]
        parts = []
        for c in range(PROJ_TN // HEAD_DIM):
            a = acc[:, c * HEAD_DIM:(c + 1) * HEAD_DIM]
            parts.append(a * cos + pltpu.roll(a, HEAD_DIM // 2, 1) * sin)
        r = jnp.concatenate(parts, axis=1)
        z_ref[...] = r
        zb_ref[...] = (r * jnp.where(j < C_KD // PROJ_TN, SCALE * LOG2E, 1.0)).astype(BF16)

    @pl.when((j >= N_ROPE_BLOCKS) & (j < N_ATTN_BLOCKS))
    def _():
        z_ref[...] = acc
        is_qs = (j >= C_QS // PROJ_TN) & (j < C_KS // PROJ_TN)
        zb_ref[...] = (acc * jnp.where(is_qs, SCALE, 1.0)).astype(BF16)

    @pl.when(j >= N_ATTN_BLOCKS)
    def _():
        z_ref[...] = acc


def _project(x, g, w_bf16, cos, sin, tm):
    t = x.shape[0]
    return pl.pallas_call(
        _proj_kernel,
        grid=(t // tm, IN_COLS // PROJ_TN),
        in_specs=[
            pl.BlockSpec((tm, D_MODEL), lambda i, j: (i, 0)),
            pl.BlockSpec((1, D_MODEL), lambda i, j: (0, 0)),
            pl.BlockSpec((D_MODEL, PROJ_TN), lambda i, j: (0, j)),
            pl.BlockSpec((tm, LANES), lambda i, j: (i, 0)),
            pl.BlockSpec((tm, LANES), lambda i, j: (i, 0)),
        ],
        out_specs=[
            pl.BlockSpec((tm, PROJ_TN), lambda i, j: (i, j)),
            pl.BlockSpec((tm, PROJ_TN), lambda i, j: (i, jnp.minimum(j, N_ATTN_BLOCKS - 1))),
        ],
        out_shape=[jax.ShapeDtypeStruct((t, IN_COLS), F32), jax.ShapeDtypeStruct((t, ATTN_COLS), BF16)],
        scratch_shapes=[pltpu.VMEM((tm, D_MODEL), BF16)],
        compiler_params=_cparams(("arbitrary", "arbitrary")),
        name="in_proj",
    )(x, g, w_bf16, cos, sin)


def _lambda(lq1, lk1, lq2, lk2):
    s1 = jnp.sum(lq1[...] * lk1[...], axis=-1, keepdims=True)
    s2 = jnp.sum(lq2[...] * lk2[...], axis=-1, keepdims=True)
    return jnp.exp(s1) - jnp.exp(s2) + LAM_INIT


def _diff_finish(acc1, l1, acc2, l2, lam, subln):
    w = acc1.shape[1]
    o = acc1 / _lanes(l1, w) - lam * (acc2 / _lanes(l2, w))
    o = o * lax.rsqrt(jnp.mean(o * o, axis=-1, keepdims=True) + RMS_EPS)
    return o * subln * (1.0 - LAM_INIT)


def _lanes(x, width):
    return x if width == LANES else jnp.concatenate([x] * (width // LANES), axis=1)


def _softmax_step(s, v, m_ref, l_ref, acc_ref, idx):
    m_old = m_ref[idx]
    m_new = jnp.maximum(m_old, jnp.max(s, axis=-1, keepdims=True))
    alpha = jnp.exp2(m_old - m_new)
    p = jnp.exp2(s - _lanes(m_new, s.shape[1]))
    l_ref[idx] = alpha * l_ref[idx] + jnp.sum(p, axis=-1, keepdims=True)
    acc_ref[idx] = _lanes(alpha, v.shape[1]) * acc_ref[idx] + _dot(p.astype(BF16), v)
    m_ref[idx] = m_new


def _strict_lower(n):
    r = lax.broadcasted_iota(jnp.int32, (n, n), 0)
    c = lax.broadcasted_iota(jnp.int32, (n, n), 1)
    return jnp.where(r > c, 1.0, 0.0).astype(BF16)


def _log_keep(z):
    return -(jnp.maximum(z, 0.0) + jnp.log1p(jnp.exp(-jnp.abs(z))))


def _split_bf16(x):
    hi = x.astype(BF16)
    return hi, (x - hi.astype(F32)).astype(BF16)


SB_CUTOFF = -104.0


def _stick_step(z, v, u, keep, r_ref, acc_ref, idx):
    lk = _log_keep(z)
    lkm = lk if keep is None else jnp.where(keep, lk, 0.0)
    hi, lo = _split_bf16(lkm)
    suffix = _dot(hi, u) + _dot(lo, u)
    r = r_ref[idx]
    a = jnp.exp(z + lk + suffix + _lanes(r, z.shape[1]))
    if keep is not None:
        a = jnp.where(keep, a, 0.0)
    acc_ref[idx] = acc_ref[idx] + _dot(a.astype(BF16), v)
    r_ref[idx] = r + jnp.sum(lkm, axis=-1, keepdims=True)


def _diff_prompt_kernel(lq1, lk1, lq2, lk2, sg_ref, q_ref, k_ref, v_ref, o_ref, m_ref, l_ref, acc_ref):
    qi = pl.program_id(1)
    tb = ATT_BLOCK
    m_ref[...] = jnp.full(m_ref.shape, NEG_INF, F32)
    l_ref[...] = jnp.zeros(l_ref.shape, F32)
    acc_ref[...] = jnp.zeros(acc_ref.shape, F32)

    def block(kb, causal):
        start = pl.multiple_of(kb * tb, tb)
        kblk = k_ref[pl.ds(start, tb), :]
        vblk = v_ref[pl.ds(start, tb), :]
        for r in range(DIFF_REP):
            for m in range(2):
                c0 = (r * 2 + m) * HEAD_DIM
                s = _dot_nt(q_ref[:, c0:c0 + HEAD_DIM], kblk[:, m * HEAD_DIM:(m + 1) * HEAD_DIM])
                if causal is not None:
                    s = jnp.where(causal, s, NEG_INF)
                _softmax_step(s, vblk, m_ref, l_ref, acc_ref, r * 2 + m)

    def body(kb, carry):
        block(kb, None)
        return carry

    lax.fori_loop(0, qi, body, 0)
    row = lax.broadcasted_iota(jnp.int32, (tb, tb), 0)
    col = lax.broadcasted_iota(jnp.int32, (tb, tb), 1)
    block(qi, col <= row)

    lam = _lambda(lq1, lk1, lq2, lk2)
    for r in range(DIFF_REP):
        o = _diff_finish(acc_ref[2 * r], l_ref[2 * r], acc_ref[2 * r + 1], l_ref[2 * r + 1], lam, sg_ref[...])
        o_ref[:, r * DIFF_V_DIM:(r + 1) * DIFF_V_DIM] = o.astype(BF16)


def _lam_specs(nd):
    zero = (lambda *a: (0, 0))
    return [pl.BlockSpec((1, HEAD_DIM), zero)] * 4 + [pl.BlockSpec((1, DIFF_V_DIM), zero)]


def _diff_prompt(zb, lams, subln):
    t = zb.shape[0]
    tb = ATT_BLOCK
    qw = DIFF_REP * 2 * HEAD_DIM
    return pl.pallas_call(
        _diff_prompt_kernel,
        grid=(DIFF_KV_HEADS, t // tb),
        in_specs=_lam_specs(2) + [
            pl.BlockSpec((tb, qw), lambda g, i: (i, g)),
            pl.BlockSpec((t, 2 * HEAD_DIM), lambda g, i: (0, C_KD // (2 * HEAD_DIM) + g)),
            pl.BlockSpec((t, DIFF_V_DIM), lambda g, i: (0, C_VD // DIFF_V_DIM + g)),
        ],
        out_specs=pl.BlockSpec((tb, DIFF_REP * DIFF_V_DIM), lambda g, i: (i, g)),
        out_shape=jax.ShapeDtypeStruct((t, D_MODEL), BF16),
        scratch_shapes=[
            pltpu.VMEM((2 * DIFF_REP, tb, LANES), F32),
            pltpu.VMEM((2 * DIFF_REP, tb, LANES), F32),
            pltpu.VMEM((2 * DIFF_REP, tb, DIFF_V_DIM), F32),
        ],
        compiler_params=_cparams(("arbitrary", "arbitrary")),
        name="diff_prompt",
    )(*lams, subln, zb, zb, zb)


def _sb_prompt_kernel(q_ref, k_ref, v_ref, o_ref, r_ref, acc_ref):
    qi = pl.program_id(1)
    tb = ATT_BLOCK
    r_ref[...] = jnp.zeros(r_ref.shape, F32)
    acc_ref[...] = jnp.zeros(acc_ref.shape, F32)
    u = _strict_lower(tb)

    def block(kb, keep):
        start = pl.multiple_of(kb * tb, tb)
        kblk = k_ref[pl.ds(start, tb), :]
        vblk = v_ref[pl.ds(start, tb), :]
        for r in range(SB_REP):
            z = _dot_nt(q_ref[:, r * HEAD_DIM:(r + 1) * HEAD_DIM], kblk)
            _stick_step(z, vblk, u, keep, r_ref, acc_ref, r)

    row = lax.broadcasted_iota(jnp.int32, (tb, tb), 0)
    col = lax.broadcasted_iota(jnp.int32, (tb, tb), 1)
    block(qi, col < row)

    def live(c):
        return (c[0] >= 0) & (c[1] > SB_CUTOFF)

    def body(c):
        block(c[0], None)
        return c[0] - 1, jnp.max(r_ref[...])

    lax.while_loop(live, body, (qi - 1, jnp.max(r_ref[...])))
    for r in range(SB_REP):
        o_ref[:, r * HEAD_DIM:(r + 1) * HEAD_DIM] = acc_ref[r].astype(BF16)


def _sb_prompt(zb):
    t = zb.shape[0]
    tb = ATT_BLOCK
    qw = SB_REP * HEAD_DIM
    return pl.pallas_call(
        _sb_prompt_kernel,
        grid=(SB_KV_HEADS, t // tb),
        in_specs=[
            pl.BlockSpec((tb, qw), lambda h, i: (i, C_QS // qw + h)),
            pl.BlockSpec((t, HEAD_DIM), lambda h, i: (0, C_KS // HEAD_DIM + h)),
            pl.BlockSpec((t, HEAD_DIM), lambda h, i: (0, C_VS // HEAD_DIM + h)),
        ],
        out_specs=pl.BlockSpec((tb, qw), lambda h, i: (i, h)),
        out_shape=jax.ShapeDtypeStruct((t, D_MODEL), BF16),
        scratch_shapes=[pltpu.VMEM((SB_REP, tb, LANES), F32), pltpu.VMEM((SB_REP, tb, HEAD_DIM), F32)],
        compiler_params=_cparams(("arbitrary", "arbitrary")),
        name="sb_prompt",
    )(zb, zb, zb)


PAGE = 128
CHUNKS = 4
Q_ROWS = 16
DEC_SEQ = 4


def _page_chunk(ref, c):
    return ref[pl.ds(c, PAGE, stride=CHUNKS), :].astype(BF16)


def _decode_diff(qd_ref, kds, vds, causal, m_ref, l_ref, acc_ref):
    n = len(kds)
    for g in range(DIFF_KV_HEADS):
        v = [[_page_chunk(vd, c * 2 + g) for c in range(2)] for vd in vds]
        for m in range(2):
            idx = g * 2 + m
            q = qd_ref[idx]
            s = [_dot_nt(q, _page_chunk(kd, idx)) for kd in kds]
            if causal is not None:
                s = [jnp.where(causal, x, NEG_INF) for x in s]
            m_old = m_ref[idx]
            m_new = jnp.maximum(m_old, jnp.max(functools.reduce(jnp.maximum, s), axis=-1, keepdims=True))
            alpha = jnp.exp2(m_old - m_new)
            p = [jnp.exp2(x - m_new) for x in s]
            l_ref[idx] = alpha * l_ref[idx] + jnp.sum(functools.reduce(jnp.add, p), axis=-1, keepdims=True)
            pb = [x.astype(BF16) for x in p]
            pv = [functools.reduce(jnp.add, [_dot(pb[i], v[i][c]) for i in range(n)]) for c in range(2)]
            acc_ref[idx] = _lanes(alpha, DIFF_V_DIM) * acc_ref[idx] + jnp.concatenate(pv, axis=1)
            m_ref[idx] = m_new


def _decode_stick(qs_ref, kss, vss, strict, u, r_ref, acc_ref):
    n = len(kss)
    rows = n * Q_ROWS
    for h in range(SB_KV_HEADS):
        q = qs_ref[h]
        z = jnp.concatenate([_dot_nt(q, _page_chunk(ks, h)) for ks in kss], axis=0)
        lk = _log_keep(z)
        keep = None if strict is None else jnp.concatenate([strict] * n, axis=0)
        lkm = lk if keep is None else jnp.where(keep, lk, 0.0)
        hi, lo = _split_bf16(lkm)
        s2 = _dot(jnp.concatenate([hi, lo], axis=0), u)
        suffix = s2[:rows] + s2[rows:]
        tot = jnp.sum(lkm, axis=-1, keepdims=True)
        r = r_ref[h]
        offs = []
        for i in range(n):
            offs.append(r)
            r = r + tot[i * Q_ROWS:(i + 1) * Q_ROWS]
        a = jnp.exp(z + lk + suffix + jnp.concatenate(offs, axis=0))
        if keep is not None:
            a = jnp.where(keep, a, 0.0)
        ab = a.astype(BF16)
        pv = [_dot(ab[i * Q_ROWS:(i + 1) * Q_ROWS], _page_chunk(vss[i], h)) for i in range(n)]
        acc_ref[h] = acc_ref[h] + functools.reduce(jnp.add, pv)
        r_ref[h] = r


def _decode_kernel(pt_ref, lq1, lk1, lq2, lk2, sg_ref, qd_ref, qs_ref, nkd, nvd, nks, nvs, *rest):
    n_in = 4 * PAGES_PER_STEP
    pages = rest[:n_in]
    od_ref, os_ref = rest[n_in:n_in + 2]
    m_ref, l_ref, accd_ref, r_ref, accs_ref = rest[n_in + 2:]
    s_id = pl.program_id(1)
    u = _strict_lower(PAGE)

    @pl.when(s_id == 0)
    def _():
        m_ref[...] = jnp.full(m_ref.shape, NEG_INF, F32)
        l_ref[...] = jnp.zeros(l_ref.shape, F32)
        accd_ref[...] = jnp.zeros(accd_ref.shape, F32)
        r_ref[...] = jnp.zeros(r_ref.shape, F32)
        accs_ref[...] = jnp.zeros(accs_ref.shape, F32)
        tq = lax.broadcasted_iota(jnp.int32, (Q_ROWS, PAGE), 0) & (DEC_SEQ - 1)
        key = lax.broadcasted_iota(jnp.int32, (Q_ROWS, PAGE), 1)
        _decode_diff(qd_ref, [nkd], [nvd], key <= tq, m_ref, l_ref, accd_ref)
        _decode_stick(qs_ref, [nks], [nvs], key < tq, u, r_ref, accs_ref)

    _decode_diff(qd_ref, pages[0::4], pages[1::4], None, m_ref, l_ref, accd_ref)

    @pl.when(jnp.max(r_ref[...]) > SB_CUTOFF)
    def _():
        _decode_stick(qs_ref, pages[2::4], pages[3::4], None, u, r_ref, accs_ref)

    @pl.when(s_id == pl.num_programs(1) - 1)
    def _():
        lam = _lambda(lq1, lk1, lq2, lk2)
        for g in range(DIFF_KV_HEADS):
            o = _diff_finish(accd_ref[2 * g], l_ref[2 * g], accd_ref[2 * g + 1], l_ref[2 * g + 1], lam, sg_ref[...])
            od_ref[g] = o.astype(BF16)
        for h in range(SB_KV_HEADS):
            os_ref[h] = accs_ref[h].astype(BF16)


def _decode(page_table, lams, subln, qd, qs, new_pages, caches):
    nb, n_pages = page_table.shape
    steps = n_pages // PAGES_PER_STEP
    zero2 = lambda b, s, pt: (0, 0)
    per_b4 = lambda b, s, pt: (b, 0, 0, 0)
    per_b3 = lambda b, s, pt: (b, 0, 0)

    def page_spec(p):
        return pl.BlockSpec((None, PAGE * CHUNKS, LANES),
                            lambda b, s, pt: (pt[b, n_pages - 1 - (s * PAGES_PER_STEP + p)], 0, 0))

    in_specs = ([pl.BlockSpec((1, HEAD_DIM), zero2)] * 4 + [pl.BlockSpec((1, DIFF_V_DIM), zero2)]
                + [pl.BlockSpec((None, 4, Q_ROWS, HEAD_DIM), per_b4)] * 2
                + [pl.BlockSpec((None, PAGE * CHUNKS, LANES), per_b3)] * 4)
    page_args = []
    for p in range(PAGES_PER_STEP):
        in_specs += [page_spec(p)] * 4
        page_args += list(caches)
    grid_spec = pltpu.PrefetchScalarGridSpec(
        num_scalar_prefetch=1,
        grid=(nb, steps),
        in_specs=in_specs,
        out_specs=[pl.BlockSpec((None, DIFF_KV_HEADS, Q_ROWS, DIFF_V_DIM), per_b4),
                   pl.BlockSpec((None, SB_KV_HEADS, Q_ROWS, HEAD_DIM), per_b4)],
        scratch_shapes=[
            pltpu.VMEM((4, Q_ROWS, LANES), F32), pltpu.VMEM((4, Q_ROWS, LANES), F32),
            pltpu.VMEM((4, Q_ROWS, DIFF_V_DIM), F32),
            pltpu.VMEM((SB_KV_HEADS, Q_ROWS, LANES), F32), pltpu.VMEM((SB_KV_HEADS, Q_ROWS, HEAD_DIM), F32),
        ],
    )
    return pl.pallas_call(
        _decode_kernel,
        grid_spec=grid_spec,
        out_shape=[jax.ShapeDtypeStruct((nb, DIFF_KV_HEADS, Q_ROWS, DIFF_V_DIM), BF16),
                   jax.ShapeDtypeStruct((nb, SB_KV_HEADS, Q_ROWS, HEAD_DIM), BF16)],
        compiler_params=_cparams(("arbitrary", "arbitrary")),
        name="decode_attn",
    )(page_table, *lams, subln, qd, qs, *new_pages, *page_args)


def _merge_kernel(od_ref, os_ref, gd_ref, gs_ref, wd_ref, ws_ref, o_ref):
    g_d = 1.0 / (1.0 + jnp.exp(-gd_ref[...]))
    g_s = 1.0 / (1.0 + jnp.exp(-gs_ref[...]))
    o_ref[...] = (g_d * _dot(od_ref[...], wd_ref[...]) + g_s * _dot(os_ref[...], ws_ref[...])).astype(BF16)


def _merge(od, os_, z32, wbd, wbs, tm):
    t = od.shape[0]
    tn = PROJ_TN
    gd0 = C_GATE // tn
    gs0 = (C_GATE + D_MODEL) // tn
    return pl.pallas_call(
        _merge_kernel,
        grid=(t // tm, D_MODEL // tn),
        in_specs=[
            pl.BlockSpec((tm, D_MODEL), lambda i, j: (i, 0)),
            pl.BlockSpec((tm, D_MODEL), lambda i, j: (i, 0)),
            pl.BlockSpec((tm, tn), lambda i, j: (i, gd0 + j)),
            pl.BlockSpec((tm, tn), lambda i, j: (i, gs0 + j)),
            pl.BlockSpec((D_MODEL, tn), lambda i, j: (0, j)),
            pl.BlockSpec((D_MODEL, tn), lambda i, j: (0, j)),
        ],
        out_specs=pl.BlockSpec((tm, tn), lambda i, j: (i, j)),
        out_shape=jax.ShapeDtypeStruct((t, D_MODEL), BF16),
        compiler_params=_cparams(("arbitrary", "arbitrary")),
        name="merge",
    )(od, os_, z32, z32, wbd, wbs)


def _out_kernel(x_ref, m_ref, w_ref, o_ref):
    o_ref[...] = x_ref[...] + _dot(m_ref[...], w_ref[...])


def _out_proj(x, merged, w_out, tm):
    t = x.shape[0]
    tn = PROJ_TN
    return pl.pallas_call(
        _out_kernel,
        grid=(t // tm, D_MODEL // tn),
        in_specs=[
            pl.BlockSpec((tm, tn), lambda i, j: (i, j)),
            pl.BlockSpec((tm, D_MODEL), lambda i, j: (i, 0)),
            pl.BlockSpec((D_MODEL, tn), lambda i, j: (0, j)),
        ],
        out_specs=pl.BlockSpec((tm, tn), lambda i, j: (i, j)),
        out_shape=jax.ShapeDtypeStruct((t, D_MODEL), F32),
        compiler_params=_cparams(("arbitrary", "arbitrary")),
        name="out_proj",
    )(x, merged, w_out)


ROW_CHUNKS = D_MODEL // LANES


def _store_row_major(ref, x):
    rows = x.shape[0]
    for c in range(ROW_CHUNKS):
        ref[pl.ds(c, rows, stride=ROW_CHUNKS), :] = x[:, c * LANES:(c + 1) * LANES]


def _load_row_major(ref, rows):
    return jnp.concatenate([ref[pl.ds(c, rows, stride=ROW_CHUNKS), :] for c in range(ROW_CHUNKS)], axis=1)


def _router_kernel(x_ref, g_ref, w_ref, b_ref, hn_ref, comb_ref, route_ref, counts_ref, carry_ref):
    hn32 = _rms(x_ref[...], g_ref[...])
    hn = hn32.astype(BF16)
    if hn_ref.dtype == BF16:
        hn_ref[...] = hn
    else:
        _store_row_major(hn_ref, hn32)
    logits = _dot(hn, w_ref[...]) + b_ref[...]
    gl = logits[:, :LANES]
    el = logits[:, LANES:]
    lane = lax.broadcasted_iota(jnp.int32, gl.shape, 1).astype(F32)
    far = float(LANES)

    def first_max(v):
        mx = jnp.max(v, axis=-1, keepdims=True)
        return mx, jnp.min(jnp.where(v == mx, lane, far), axis=-1, keepdims=True)

    gl = jnp.where(lane < N_GROUPS, gl, NEG_INF)
    gmax, grp = first_max(gl)
    p_grp = 1.0 / jnp.sum(jnp.exp(gl - gmax), axis=-1, keepdims=True)
    lo = grp * EXPERTS_PER_GROUP
    es = jnp.where((lane >= lo) & (lane < lo + EXPERTS_PER_GROUP), el, NEG_INF)
    v1, i1 = first_max(es)
    es2 = jnp.where(lane == i1, NEG_INF, es)
    v2, i2 = first_max(es2)
    e2 = jnp.exp(v2 - v1)
    w1 = 1.0 / (1.0 + e2)
    w2 = e2 / (1.0 + e2)
    ga = p_grp * w1
    gb = p_grp * w2
    oh_a = jnp.where(lane == i1, 1.0, 0.0)
    oh_b = jnp.where(lane == i2, 1.0, 0.0)
    comb_ref[...] = ga * oh_a + gb * oh_b

    @pl.when(pl.program_id(0) == 0)
    def _():
        carry_ref[...] = jnp.zeros(carry_ref.shape, F32)

    tm = x_ref.shape[0]
    oh = oh_a + oh_b
    before = _dot(_strict_lower(tm), oh.astype(BF16)) + carry_ref[...]
    rank_a = jnp.sum(before * oh_a, axis=-1, keepdims=True)
    rank_b = jnp.sum(before * oh_b, axis=-1, keepdims=True)
    carry_ref[...] = carry_ref[...] + jnp.sum(oh, axis=0, keepdims=True)
    counts_ref[...] = carry_ref[...]
    fields = (i1, i2, ga, gb, rank_a, rank_b)
    route_ref[...] = functools.reduce(jnp.add, [jnp.where(lane == float(k), f, 0.0) for k, f in enumerate(fields)])


def _router(x2, g, w_r, b_r, tm, routed):
    t = x2.shape[0]
    if routed:
        hn_spec = pl.BlockSpec((tm * ROW_CHUNKS, LANES), lambda i: (i, 0))
        hn_shape = jax.ShapeDtypeStruct((t * ROW_CHUNKS, LANES), F32)
    else:
        hn_spec = pl.BlockSpec((tm, D_MODEL), lambda i: (i, 0))
        hn_shape = jax.ShapeDtypeStruct((t, D_MODEL), BF16)
    return pl.pallas_call(
        _router_kernel,
        grid=(t // tm,),
        in_specs=[
            pl.BlockSpec((tm, D_MODEL), lambda i: (i, 0)),
            pl.BlockSpec((1, D_MODEL), lambda i: (0, 0)),
            pl.BlockSpec((D_MODEL, 2 * LANES), lambda i: (0, 0)),
            pl.BlockSpec((1, 2 * LANES), lambda i: (0, 0)),
        ],
        out_specs=[hn_spec, pl.BlockSpec((tm, LANES), lambda i: (i, 0)),
                   pl.BlockSpec((tm, LANES), lambda i: (i, 0)), pl.BlockSpec((1, LANES), lambda i: (0, 0))],
        out_shape=[hn_shape, jax.ShapeDtypeStruct((t, LANES), F32),
                   jax.ShapeDtypeStruct((t, LANES), F32), jax.ShapeDtypeStruct((1, LANES), F32)],
        scratch_shapes=[pltpu.VMEM((1, LANES), F32)],
        compiler_params=_cparams(("arbitrary",)),
        name="router",
    )(x2, g, w_r, b_r)


MOE_TM = 256


DMA_UNROLL = 8


def _moe_routed_kernel(te_ref, dst_ref, wg_ref, wu_ref, wd_ref, hn_hbm, y_hbm, xbuf, ybuf, gsem, ssem, *, n):
    j = pl.program_id(0)

    def row_in(r):
        d = dst_ref[0, r]
        src = jnp.where(d < 0, 0, jnp.where(d >= n, d - n, d))
        return pltpu.make_async_copy(hn_hbm.at[pl.ds(pl.multiple_of(src * ROW_CHUNKS, ROW_CHUNKS), ROW_CHUNKS), :],
                                     xbuf.at[pl.ds(pl.multiple_of(r * ROW_CHUNKS, ROW_CHUNKS), ROW_CHUNKS), :], gsem)

    def row_out(r):
        d = dst_ref[0, r]
        return pltpu.make_async_copy(ybuf.at[pl.ds(pl.multiple_of(r * ROW_CHUNKS, ROW_CHUNKS), ROW_CHUNKS), :],
                                     y_hbm.at[pl.ds(pl.multiple_of(d * ROW_CHUNKS, ROW_CHUNKS), ROW_CHUNKS), :], ssem)

    def each_row(fn):
        def body(r, c):
            fn(r)
            return c
        lax.fori_loop(0, MOE_TM, body, 0, unroll=DMA_UNROLL)

    @pl.when(te_ref[j] < N_EXPERTS)
    def _():
        each_row(lambda r: row_in(r).start())
        each_row(lambda r: row_in(r).wait())
        x = _load_row_major(xbuf, MOE_TM).astype(BF16)
        gate = _dot(x, wg_ref[...])
        hid = gate * (1.0 / (1.0 + jnp.exp(-gate))) * _dot(x, wu_ref[...])
        _store_row_major(ybuf, _dot(hid.astype(BF16), wd_ref[...]))
        each_row(lambda r: pl.when(dst_ref[0, r] >= 0)(lambda: row_out(r).start()))
        each_row(lambda r: pl.when(dst_ref[0, r] >= 0)(lambda: row_out(r).wait()))


def _moe_routed(hn, route, counts, wg, wu, wd):
    n = hn.shape[0] // ROW_CHUNKS
    n_tiles = (2 * n) // MOE_TM + N_EXPERTS
    p = n_tiles * MOE_TM
    experts = route[:, 0:2].astype(jnp.int32)
    ranks = route[:, 4:6].astype(jnp.int32)
    cnt = counts[0, :N_EXPERTS].astype(jnp.int32)
    padded = (cnt + MOE_TM - 1) // MOE_TM * MOE_TM
    ends = jnp.cumsum(padded)
    slot = ((ends - padded)[experts] + ranks).reshape(-1)
    tok = jnp.arange(n, dtype=jnp.int32)
    dst = jnp.full((p,), -1, jnp.int32).at[slot].set(jnp.stack([tok, tok + n], axis=1).reshape(-1))
    tile_expert = jnp.sum(jnp.arange(n_tiles, dtype=jnp.int32)[:, None] * MOE_TM >= ends[None, :], axis=1)
    tile_expert = tile_expert.astype(jnp.int32)

    def w_spec(shape):
        return pl.BlockSpec((None,) + shape, lambda j, te: (jnp.minimum(te[j], N_EXPERTS - 1), 0, 0))

    grid_spec = pltpu.PrefetchScalarGridSpec(
        num_scalar_prefetch=1,
        grid=(n_tiles,),
        in_specs=[pl.BlockSpec((None, 1, MOE_TM), lambda j, te: (j, 0, 0), memory_space=pltpu.SMEM),
                  w_spec((D_MODEL, D_EXPERT)), w_spec((D_MODEL, D_EXPERT)), w_spec((D_EXPERT, D_MODEL)),
                  pl.BlockSpec(memory_space=pl.ANY)],
        out_specs=pl.BlockSpec(memory_space=pl.ANY),
        scratch_shapes=[pltpu.VMEM((MOE_TM * ROW_CHUNKS, LANES), F32), pltpu.VMEM((MOE_TM * ROW_CHUNKS, LANES), F32),
                        pltpu.SemaphoreType.DMA(()), pltpu.SemaphoreType.DMA(())],
    )
    return pl.pallas_call(
        functools.partial(_moe_routed_kernel, n=n),
        grid_spec=grid_spec,
        out_shape=jax.ShapeDtypeStruct((2 * n * ROW_CHUNKS, LANES), F32),
        compiler_params=_cparams(("arbitrary",)),
        name="moe_routed",
    )(tile_expert, dst.reshape(n_tiles, 1, MOE_TM), wg, wu, wd, hn)


def _combine_kernel(x_ref, route_ref, ya_ref, yb_ref, g_ref, o_ref):
    tm = x_ref.shape[0]
    route = route_ref[...]
    lane = lax.broadcasted_iota(jnp.int32, route.shape, 1)
    ga = jnp.sum(jnp.where(lane == 2, route, 0.0), axis=-1, keepdims=True)
    gb = jnp.sum(jnp.where(lane == 3, route, 0.0), axis=-1, keepdims=True)
    moe = ga * _load_row_major(ya_ref, tm) + gb * _load_row_major(yb_ref, tm)
    o_ref[...] = _rms(x_ref[...] + moe, g_ref[...])


def _combine(x2, route, y2, gf, tm):
    t = x2.shape[0]
    nblk = t // tm
    return pl.pallas_call(
        _combine_kernel,
        grid=(nblk,),
        in_specs=[pl.BlockSpec((tm, D_MODEL), lambda i: (i, 0)),
                  pl.BlockSpec((tm, LANES), lambda i: (i, 0)),
                  pl.BlockSpec((tm * ROW_CHUNKS, LANES), lambda i: (i, 0)),
                  pl.BlockSpec((tm * ROW_CHUNKS, LANES), lambda i: (i + nblk, 0)),
                  pl.BlockSpec((1, D_MODEL), lambda i: (0, 0))],
        out_specs=pl.BlockSpec((tm, D_MODEL), lambda i: (i, 0)),
        out_shape=jax.ShapeDtypeStruct((t, D_MODEL), F32),
        compiler_params=_cparams(("arbitrary",)),
        name="moe_combine",
    )(x2, route, y2, y2, gf)


def _moe_kernel(hn_ref, comb_ref, x_ref, wg_ref, wu_ref, wd_ref, gf_ref, y_ref):
    e = pl.program_id(1)

    @pl.when(e == 0)
    def _():
        y_ref[...] = x_ref[...]

    hn = hn_ref[...]
    lane = lax.broadcasted_iota(jnp.int32, comb_ref.shape, 1)
    c = jnp.sum(jnp.where(lane == e, comb_ref[...], 0.0), axis=-1, keepdims=True)
    gate = _dot(hn, wg_ref[...])
    hid = gate * (1.0 / (1.0 + jnp.exp(-gate))) * _dot(hn, wu_ref[...])
    y_ref[...] += _dot((hid * c).astype(BF16), wd_ref[...])

    @pl.when(e == pl.num_programs(1) - 1)
    def _():
        y_ref[...] = _rms(y_ref[...], gf_ref[...])


def _moe(hn, comb, x2, wg, wu, wd, gf, tm):
    t = hn.shape[0]
    return pl.pallas_call(
        _moe_kernel,
        grid=(t // tm, N_EXPERTS),
        in_specs=[
            pl.BlockSpec((tm, D_MODEL), lambda i, e: (i, 0)),
            pl.BlockSpec((tm, LANES), lambda i, e: (i, 0)),
            pl.BlockSpec((tm, D_MODEL), lambda i, e: (i, 0)),
            pl.BlockSpec((None, D_MODEL, D_EXPERT), lambda i, e: (e, 0, 0)),
            pl.BlockSpec((None, D_MODEL, D_EXPERT), lambda i, e: (e, 0, 0)),
            pl.BlockSpec((None, D_EXPERT, D_MODEL), lambda i, e: (e, 0, 0)),
            pl.BlockSpec((1, D_MODEL), lambda i, e: (0, 0)),
        ],
        out_specs=pl.BlockSpec((tm, D_MODEL), lambda i, e: (i, 0)),
        out_shape=jax.ShapeDtypeStruct((t, D_MODEL), F32),
        compiler_params=_cparams(("arbitrary", "arbitrary")),
        name="moe",
    )(hn, comb, x2, wg, wu, wd, gf)


def _post_attention(x, od, os_, z32, w, tm):
    merged = _merge(od, os_, z32, w["wbd"], w["wbs"], tm)
    x2 = _out_proj(x, merged, w["wout"], tm)
    if x.shape[0] // MOE_TM < N_EXPERTS:
        hn, comb, _, _ = _router(x2, w["g_ffn"], w["w_r"], w["b_r"], tm, False)
        return _moe(hn, comb, x2, w["wg"], w["wu"], w["wd"], w["g_final"], tm)
    hn, _, route, counts = _router(x2, w["g_ffn"], w["w_r"], w["b_r"], tm, True)
    y2 = _moe_routed(hn, route, counts, w["wg"], w["wu"], w["wd"])
    return _combine(x2, route, y2, w["g_final"], tm)


def _pad_rows(a, rows):
    return jnp.pad(a, ((0, rows - a.shape[0]),) + ((0, 0),) * (a.ndim - 1))


def kernel(x_prompt, x_sample, cache_k_diff, cache_v_diff, cache_k_sb, cache_v_sb, page_table, meta_tokens, norm_mix_g, w_in, lambda_q1, lambda_k1, lambda_q2, lambda_k2, subln_g, w_branch_diff, w_branch_sb, w_out, norm_ffn_g, w_router_group, b_router_group, w_router_expert, b_router_expert, w_expert_gate, w_expert_up, w_expert_down, norm_final_g):
    assert x_prompt.shape[0] == 1 and norm_mix_g.shape[0] == 1
    seq = x_prompt.shape[1]
    tp_real = N_META + seq
    tp = -(-tp_real // ROW_BLOCK) * ROW_BLOCK
    assert tp % ATT_BLOCK == 0
    nb, dec_seq, _ = x_sample.shape
    ts = nb * dec_seq
    n_phys = cache_k_diff.shape[1]
    past_len = page_table.shape[1] * PAGE
    assert dec_seq == DEC_SEQ and cache_k_diff.shape[2] == PAGE

    w_in_b = w_in[0].astype(BF16)
    w_r = jnp.zeros((D_MODEL, 2 * LANES), F32)
    w_r = w_r.at[:, :N_GROUPS].set(w_router_group[0]).at[:, LANES:LANES + N_EXPERTS].set(w_router_expert[0])
    b_r = jnp.zeros((1, 2 * LANES), F32)
    b_r = b_r.at[0, :N_GROUPS].set(b_router_group[0]).at[0, LANES:LANES + N_EXPERTS].set(b_router_expert[0])
    w = dict(
        wbd=w_branch_diff[0].astype(BF16), wbs=w_branch_sb[0].astype(BF16), wout=w_out[0].astype(BF16),
        g_ffn=norm_ffn_g, w_r=w_r.astype(BF16), b_r=b_r,
        wg=w_expert_gate[0].astype(BF16), wu=w_expert_up[0].astype(BF16), wd=w_expert_down[0].astype(BF16),
        g_final=norm_final_g.reshape(1, D_MODEL),
    )
    lams = (lambda_q1, lambda_k1, lambda_q2, lambda_k2)

    half = HEAD_DIM // 2
    inv = ROPE_THETA ** (-jnp.arange(half, dtype=F32) * 2.0 / HEAD_DIM)
    inv = jnp.concatenate([inv, inv]).reshape(1, LANES)

    xp = _pad_rows(jnp.concatenate([meta_tokens.astype(F32), x_prompt[0]], axis=0), tp)
    pos_p = jnp.arange(tp, dtype=jnp.int32).astype(F32).reshape(tp, 1)
    cos_p, sin_p = _rope_tables(pos_p, inv, ROW_BLOCK)
    z32_p, zb_p = _project(xp, norm_mix_g, w_in_b, cos_p, sin_p, ROW_BLOCK)
    od_p = _diff_prompt(zb_p, lams, subln_g)
    os_p = _sb_prompt(zb_p)
    y_p = _post_attention(xp, od_p, os_p, z32_p, w, ROW_BLOCK)

    xs = x_sample.reshape(ts, D_MODEL)
    pos_s = jnp.tile(past_len + jnp.arange(dec_seq, dtype=jnp.int32), nb).astype(F32).reshape(ts, 1)
    cos_s, sin_s = _rope_tables(pos_s, inv, ts)
    z32_s, zb_s = _project(xs, norm_mix_g, w_in_b, cos_s, sin_s, ts)

    qd = zb_s[:, C_QD:C_KD].reshape(nb, dec_seq, DIFF_KV_HEADS, DIFF_REP, 2, HEAD_DIM)
    qd = qd.transpose(0, 2, 4, 3, 1, 5).reshape(nb, 4, Q_ROWS, HEAD_DIM)
    qs = zb_s[:, C_QS:C_KS].reshape(nb, dec_seq, SB_KV_HEADS, SB_REP, HEAD_DIM)
    qs = qs.transpose(0, 2, 3, 1, 4).reshape(nb, SB_KV_HEADS, Q_ROWS, HEAD_DIM)

    def new_page(a):
        return jnp.pad(a.reshape(nb, dec_seq * CHUNKS, LANES), ((0, 0), (0, (PAGE - dec_seq) * CHUNKS), (0, 0)))

    nvd = z32_s[:, C_VD:C_QS].reshape(nb, dec_seq, DIFF_KV_HEADS, 2, LANES).transpose(0, 1, 3, 2, 4)
    new_pages = (new_page(z32_s[:, C_KD:C_VD]), new_page(nvd),
                 new_page(z32_s[:, C_KS:C_VS]), new_page(z32_s[:, C_VS:C_GATE]))
    cvd = cache_v_diff[0].reshape(n_phys, PAGE, DIFF_KV_HEADS, 2, LANES).transpose(0, 1, 3, 2, 4)
    caches = (cache_k_diff[0].reshape(n_phys, PAGE * CHUNKS, LANES), cvd.reshape(n_phys, PAGE * CHUNKS, LANES),
              cache_k_sb[0].reshape(n_phys, PAGE * CHUNKS, LANES), cache_v_sb[0].reshape(n_phys, PAGE * CHUNKS, LANES))
    od_s, os_s = _decode(page_table, lams, subln_g, qd, qs, new_pages, caches)
    od_s = od_s.reshape(nb, DIFF_KV_HEADS, DIFF_REP, dec_seq, DIFF_V_DIM).transpose(0, 3, 1, 2, 4).reshape(ts, D_MODEL)
    os_s = os_s.reshape(nb, SB_KV_HEADS, SB_REP, dec_seq, HEAD_DIM).transpose(0, 3, 1, 2, 4).reshape(ts, D_MODEL)
    y_s = _post_attention(xs, od_s, os_s, z32_s, w, ts)

    def kv(z, rows, lead):
        return (z[:rows, C_KD:C_VD].reshape(lead + (DIFF_KV_HEADS, 2, HEAD_DIM)),
                z[:rows, C_VD:C_QS].reshape(lead + (DIFF_KV_HEADS, DIFF_V_DIM)),
                z[:rows, C_KS:C_VS].reshape(lead + (SB_KV_HEADS, HEAD_DIM)),
                z[:rows, C_VS:C_GATE].reshape(lead + (SB_KV_HEADS, HEAD_DIM)))

    y_prompt = y_p[N_META:tp_real].reshape(1, seq, D_MODEL)
    y_sample = y_s.reshape(nb, dec_seq, D_MODEL)
    return (y_prompt, y_sample) + kv(z32_p, tp_real, (1, 1, tp_real)) + kv(z32_s, ts, (1, nb, dec_seq))
```

```python
import functools
import math

import jax
import jax.numpy as jnp
import numpy as np
from jax import lax
from jax.experimental import pallas as pl
from jax.experimental.pallas import tpu as pltpu

F32 = jnp.float32
BF16 = jnp.bfloat16

D_MODEL = 2048
N_META = 16
HEAD_DIM = 128
DIFF_KV_HEADS = 2
DIFF_REP = 4
DIFF_V_DIM = 2 * HEAD_DIM
SB_KV_HEADS = 4
SB_REP = 4
ROPE_THETA = 10000.0
N_GROUPS = 4
EXPERTS_PER_GROUP = 8
N_EXPERTS = N_GROUPS * EXPERTS_PER_GROUP
D_EXPERT = 512
RMS_EPS = 1e-6
NEG_INF = -1e30
LAM_INIT = 0.8 - 0.6 * math.exp(-0.3 * 0)
SCALE = HEAD_DIM ** -0.5
LOG2E = math.log2(math.e)

C_QD, C_KD, C_VD, C_QS, C_KS, C_VS, C_GATE = 0, 2048, 2560, 3072, 5120, 5632, 6144
IN_COLS = 10240
ATTN_COLS = C_GATE

LANES = 128
PROJ_TN = 512
ROW_BLOCK = 768
ATT_BLOCK = 256
PAGES_PER_STEP = 16
VMEM_LIMIT = 56 * 1024 * 1024


def _cparams(sem):
    return pltpu.CompilerParams(dimension_semantics=sem, vmem_limit_bytes=VMEM_LIMIT)


def _dot(a, b):
    return jnp.dot(a, b, preferred_element_type=F32)


def _dot_nt(a, b):
    return lax.dot_general(a, b, (((1,), (1,)), ((), ())), preferred_element_type=F32)


def _rms(x, g):
    return x * lax.rsqrt(jnp.mean(x * x, axis=-1, keepdims=True) + RMS_EPS) * g


def _rope_table_kernel(pos_ref, inv_ref, cos_ref, sin_ref):
    ang = pos_ref[...] * inv_ref[...]
    lane = lax.broadcasted_iota(jnp.int32, ang.shape, 1)
    s = jnp.sin(ang)
    cos_ref[...] = jnp.cos(ang)
    sin_ref[...] = jnp.where(lane < HEAD_DIM // 2, -s, s)


def _rope_tables(pos, inv, tm):
    t = pos.shape[0]
    return pl.pallas_call(
        _rope_table_kernel,
        grid=(t // tm,),
        in_specs=[pl.BlockSpec((tm, 1), lambda i: (i, 0)), pl.BlockSpec((1, LANES), lambda i: (0, 0))],
        out_specs=[pl.BlockSpec((tm, LANES), lambda i: (i, 0))] * 2,
        out_shape=[jax.ShapeDtypeStruct((t, LANES), F32)] * 2,
        compiler_params=_cparams(("arbitrary",)),
        name="rope_tables",
    )(pos, inv)


N_ROPE_BLOCKS = C_VD // PROJ_TN
N_ATTN_BLOCKS = ATTN_COLS // PROJ_TN


def _proj_kernel(x_ref, g_ref, w_ref, cos_ref, sin_ref, z_ref, zb_ref, h_ref):
    j = pl.program_id(1)

    @pl.when(j == 0)
    def _():
        h_ref[...] = _rms(x_ref[...], g_ref[...]).astype(BF16)

    acc = _dot(h_ref[...], w_ref[...])

    @pl.when(j < N_ROPE_BLOCKS)
    def _():
        cos = cos_ref[...]
        sin = sin_ref[...]
        parts = []
        for c in range(PROJ_TN // HEAD_DIM):
            a = acc[:, c * HEAD_DIM:(c + 1) * HEAD_DIM]
            parts.append(a * cos + pltpu.roll(a, HEAD_DIM // 2, 1) * sin)
        r = jnp.concatenate(parts, axis=1)
        z_ref[...] = r
        zb_ref[...] = (r * jnp.where(j < C_KD // PROJ_TN, SCALE * LOG2E, 1.0)).astype(BF16)

    @pl.when((j >= N_ROPE_BLOCKS) & (j < N_ATTN_BLOCKS))
    def _():
        z_ref[...] = acc
        is_qs = (j >= C_QS // PROJ_TN) & (j < C_KS // PROJ_TN)
        zb_ref[...] = (acc * jnp.where(is_qs, SCALE, 1.0)).astype(BF16)

    @pl.when(j >= N_ATTN_BLOCKS)
    def _():
        z_ref[...] = acc


def _project(x, g, w_bf16, cos, sin, tm):
    t = x.shape[0]
    return pl.pallas_call(
        _proj_kernel,
        grid=(t // tm, IN_COLS // PROJ_TN),
        in_specs=[
            pl.BlockSpec((tm, D_MODEL), lambda i, j: (i, 0)),
            pl.BlockSpec((1, D_MODEL), lambda i, j: (0, 0)),
            pl.BlockSpec((D_MODEL, PROJ_TN), lambda i, j: (0, j)),
            pl.BlockSpec((tm, LANES), lambda i, j: (i, 0)),
            pl.BlockSpec((tm, LANES), lambda i, j: (i, 0)),
        ],
        out_specs=[
            pl.BlockSpec((tm, PROJ_TN), lambda i, j: (i, j)),
            pl.BlockSpec((tm, PROJ_TN), lambda i, j: (i, jnp.minimum(j, N_ATTN_BLOCKS - 1))),
        ],
        out_shape=[jax.ShapeDtypeStruct((t, IN_COLS), F32), jax.ShapeDtypeStruct((t, ATTN_COLS), BF16)],
        scratch_shapes=[pltpu.VMEM((tm, D_MODEL), BF16)],
        compiler_params=_cparams(("arbitrary", "arbitrary")),
        name="in_proj",
    )(x, g, w_bf16, cos, sin)


def _lambda(lq1, lk1, lq2, lk2):
    s1 = jnp.sum(lq1[...] * lk1[...], axis=-1, keepdims=True)
    s2 = jnp.sum(lq2[...] * lk2[...], axis=-1, keepdims=True)
    return jnp.exp(s1) - jnp.exp(s2) + LAM_INIT


def _diff_finish(acc1, l1, acc2, l2, lam, subln):
    w = acc1.shape[1]
    o = acc1 / _lanes(l1, w) - lam * (acc2 / _lanes(l2, w))
    o = o * lax.rsqrt(jnp.mean(o * o, axis=-1, keepdims=True) + RMS_EPS)
    return o * subln * (1.0 - LAM_INIT)


def _lanes(x, width):
    return x if width == LANES else jnp.concatenate([x] * (width // LANES), axis=1)


def _softmax_step(s, v, m_ref, l_ref, acc_ref, idx):
    m_old = m_ref[idx]
    m_new = jnp.maximum(m_old, jnp.max(s, axis=-1, keepdims=True))
    alpha = jnp.exp2(m_old - m_new)
    p = jnp.exp2(s - _lanes(m_new, s.shape[1]))
    l_ref[idx] = alpha * l_ref[idx] + jnp.sum(p, axis=-1, keepdims=True)
    acc_ref[idx] = _lanes(alpha, v.shape[1]) * acc_ref[idx] + _dot(p.astype(BF16), v)
    m_ref[idx] = m_new


def _strict_lower(n):
    r = lax.broadcasted_iota(jnp.int32, (n, n), 0)
    c = lax.broadcasted_iota(jnp.int32, (n, n), 1)
    return jnp.where(r > c, 1.0, 0.0).astype(BF16)


def _log_keep(z):
    return -(jnp.maximum(z, 0.0) + jnp.log(1.0 + jnp.exp(-jnp.abs(z))))


def _split_bf16(x):
    hi = x.astype(BF16)
    return hi, (x - hi.astype(F32)).astype(BF16)


SB_CUTOFF = -104.0


def _stick_step(z, v, u, keep, r_ref, acc_ref, idx):
    lk = _log_keep(z)
    lkm = lk if keep is None else jnp.where(keep, lk, 0.0)
    hi, lo = _split_bf16(lkm)
    suffix = _dot(hi, u) + _dot(lo, u)
    r = r_ref[idx]
    a = jnp.exp(z + lk + suffix + _lanes(r, z.shape[1]))
    if keep is not None:
        a = jnp.where(keep, a, 0.0)
    acc_ref[idx] = acc_ref[idx] + _dot(a.astype(BF16), v)
    r_ref[idx] = r + jnp.sum(lkm, axis=-1, keepdims=True)


def _diff_prompt_kernel(lq1, lk1, lq2, lk2, sg_ref, q_ref, k_ref, v_ref, o_ref, m_ref, l_ref, acc_ref):
    qi = pl.program_id(1)
    tb = ATT_BLOCK
    m_ref[...] = jnp.full(m_ref.shape, NEG_INF, F32)
    l_ref[...] = jnp.zeros(l_ref.shape, F32)
    acc_ref[...] = jnp.zeros(acc_ref.shape, F32)

    def block(kb, causal):
        start = pl.multiple_of(kb * tb, tb)
        kblk = k_ref[pl.ds(start, tb), :]
        vblk = v_ref[pl.ds(start, tb), :]
        for r in range(DIFF_REP):
            for m in range(2):
                c0 = (r * 2 + m) * HEAD_DIM
                s = _dot_nt(q_ref[:, c0:c0 + HEAD_DIM], kblk[:, m * HEAD_DIM:(m + 1) * HEAD_DIM])
                if causal is not None:
                    s = jnp.where(causal, s, NEG_INF)
                _softmax_step(s, vblk, m_ref, l_ref, acc_ref, r * 2 + m)

    def body(kb, carry):
        block(kb, None)
        return carry

    lax.fori_loop(0, qi, body, 0)
    row = lax.broadcasted_iota(jnp.int32, (tb, tb), 0)
    col = lax.broadcasted_iota(jnp.int32, (tb, tb), 1)
    block(qi, col <= row)

    lam = _lambda(lq1, lk1, lq2, lk2)
    for r in range(DIFF_REP):
        o = _diff_finish(acc_ref[2 * r], l_ref[2 * r], acc_ref[2 * r + 1], l_ref[2 * r + 1], lam, sg_ref[...])
        o_ref[:, r * DIFF_V_DIM:(r + 1) * DIFF_V_DIM] = o.astype(BF16)


def _lam_specs(nd):
    zero = (lambda *a: (0, 0))
    return [pl.BlockSpec((1, HEAD_DIM), zero)] * 4 + [pl.BlockSpec((1, DIFF_V_DIM), zero)]


def _diff_prompt(zb, lams, subln):
    t = zb.shape[0]
    tb = ATT_BLOCK
    qw = DIFF_REP * 2 * HEAD_DIM
    return pl.pallas_call(
        _diff_prompt_kernel,
        grid=(DIFF_KV_HEADS, t // tb),
        in_specs=_lam_specs(2) + [
            pl.BlockSpec((tb, qw), lambda g, i: (i, g)),
            pl.BlockSpec((t, 2 * HEAD_DIM), lambda g, i: (0, C_KD // (2 * HEAD_DIM) + g)),
            pl.BlockSpec((t, DIFF_V_DIM), lambda g, i: (0, C_VD // DIFF_V_DIM + g)),
        ],
        out_specs=pl.BlockSpec((tb, DIFF_REP * DIFF_V_DIM), lambda g, i: (i, g)),
        out_shape=jax.ShapeDtypeStruct((t, D_MODEL), BF16),
        scratch_shapes=[
            pltpu.VMEM((2 * DIFF_REP, tb, LANES), F32),
            pltpu.VMEM((2 * DIFF_REP, tb, LANES), F32),
            pltpu.VMEM((2 * DIFF_REP, tb, DIFF_V_DIM), F32),
        ],
        compiler_params=_cparams(("arbitrary", "arbitrary")),
        name="diff_prompt",
    )(*lams, subln, zb, zb, zb)


def _sb_prompt_kernel(q_ref, k_ref, v_ref, o_ref, r_ref, acc_ref):
    qi = pl.program_id(1)
    tb = ATT_BLOCK
    r_ref[...] = jnp.zeros(r_ref.shape, F32)
    acc_ref[...] = jnp.zeros(acc_ref.shape, F32)
    u = _strict_lower(tb)

    def block(kb, keep):
        start = pl.multiple_of(kb * tb, tb)
        kblk = k_ref[pl.ds(start, tb), :]
        vblk = v_ref[pl.ds(start, tb), :]
        for r in range(SB_REP):
            z = _dot_nt(q_ref[:, r * HEAD_DIM:(r + 1) * HEAD_DIM], kblk)
            _stick_step(z, vblk, u, keep, r_ref, acc_ref, r)

    row = lax.broadcasted_iota(jnp.int32, (tb, tb), 0)
    col = lax.broadcasted_iota(jnp.int32, (tb, tb), 1)
    block(qi, col < row)

    def live(c):
        return (c[0] >= 0) & (c[1] > SB_CUTOFF)

    def body(c):
        block(c[0], None)
        return c[0] - 1, jnp.max(r_ref[...])

    lax.while_loop(live, body, (qi - 1, jnp.max(r_ref[...])))
    for r in range(SB_REP):
        o_ref[:, r * HEAD_DIM:(r + 1) * HEAD_DIM] = acc_ref[r].astype(BF16)


def _sb_prompt(zb):
    t = zb.shape[0]
    tb = ATT_BLOCK
    qw = SB_REP * HEAD_DIM
    return pl.pallas_call(
        _sb_prompt_kernel,
        grid=(SB_KV_HEADS, t // tb),
        in_specs=[
            pl.BlockSpec((tb, qw), lambda h, i: (i, C_QS // qw + h)),
            pl.BlockSpec((t, HEAD_DIM), lambda h, i: (0, C_KS // HEAD_DIM + h)),
            pl.BlockSpec((t, HEAD_DIM), lambda h, i: (0, C_VS // HEAD_DIM + h)),
        ],
        out_specs=pl.BlockSpec((tb, qw), lambda h, i: (i, h)),
        out_shape=jax.ShapeDtypeStruct((t, D_MODEL), BF16),
        scratch_shapes=[pltpu.VMEM((SB_REP, tb, LANES), F32), pltpu.VMEM((SB_REP, tb, HEAD_DIM), F32)],
        compiler_params=_cparams(("arbitrary", "arbitrary")),
        name="sb_prompt",
    )(zb, zb, zb)


PAGE = 128
CHUNKS = 4
Q_ROWS = 16
DEC_SEQ = 4


def _page_chunk(ref, c):
    return ref[pl.ds(c, PAGE, stride=CHUNKS), :].astype(BF16)


def _decode_diff(qd_ref, kds, vds, causal, m_ref, l_ref, acc_ref):
    n = len(kds)
    for g in range(DIFF_KV_HEADS):
        v = [[_page_chunk(vd, c * 2 + g) for c in range(2)] for vd in vds]
        for m in range(2):
            idx = g * 2 + m
            q = qd_ref[idx]
            s = [_dot_nt(q, _page_chunk(kd, idx)) for kd in kds]
            if causal is not None:
                s = [jnp.where(causal, x, NEG_INF) for x in s]
            m_old = m_ref[idx]
            m_new = jnp.maximum(m_old, jnp.max(functools.reduce(jnp.maximum, s), axis=-1, keepdims=True))
            alpha = jnp.exp2(m_old - m_new)
            p = [jnp.exp2(x - m_new) for x in s]
            l_ref[idx] = alpha * l_ref[idx] + jnp.sum(functools.reduce(jnp.add, p), axis=-1, keepdims=True)
            pb = [x.astype(BF16) for x in p]
            pv = [functools.reduce(jnp.add, [_dot(pb[i], v[i][c]) for i in range(n)]) for c in range(2)]
            acc_ref[idx] = _lanes(alpha, DIFF_V_DIM) * acc_ref[idx] + jnp.concatenate(pv, axis=1)
            m_ref[idx] = m_new


def _decode_stick(qs_ref, kss, vss, strict, u, r_ref, acc_ref):
    n = len(kss)
    rows = n * Q_ROWS
    for h in range(SB_KV_HEADS):
        q = qs_ref[h]
        z = jnp.concatenate([_dot_nt(q, _page_chunk(ks, h)) for ks in kss], axis=0)
        lk = _log_keep(z)
        keep = None if strict is None else jnp.concatenate([strict] * n, axis=0)
        lkm = lk if keep is None else jnp.where(keep, lk, 0.0)
        hi, lo = _split_bf16(lkm)
        s2 = _dot(jnp.concatenate([hi, lo], axis=0), u)
        suffix = s2[:rows] + s2[rows:]
        tot = jnp.sum(lkm, axis=-1, keepdims=True)
        r = r_ref[h]
        offs = []
        for i in range(n):
            offs.append(r)
            r = r + tot[i * Q_ROWS:(i + 1) * Q_ROWS]
        a = jnp.exp(z + lk + suffix + jnp.concatenate(offs, axis=0))
        if keep is not None:
            a = jnp.where(keep, a, 0.0)
        ab = a.astype(BF16)
        pv = [_dot(ab[i * Q_ROWS:(i + 1) * Q_ROWS], _page_chunk(vss[i], h)) for i in range(n)]
        acc_ref[h] = acc_ref[h] + functools.reduce(jnp.add, pv)
        r_ref[h] = r


def _decode_kernel(pt_ref, lq1, lk1, lq2, lk2, sg_ref, qd_ref, qs_ref, nkd, nvd, nks, nvs, *rest):
    n_in = 4 * PAGES_PER_STEP
    pages = rest[:n_in]
    od_ref, os_ref = rest[n_in:n_in + 2]
    m_ref, l_ref, accd_ref, r_ref, accs_ref = rest[n_in + 2:]
    s_id = pl.program_id(1)
    u = _strict_lower(PAGE)

    @pl.when(s_id == 0)
    def _():
        m_ref[...] = jnp.full(m_ref.shape, NEG_INF, F32)
        l_ref[...] = jnp.zeros(l_ref.shape, F32)
        accd_ref[...] = jnp.zeros(accd_ref.shape, F32)
        r_ref[...] = jnp.zeros(r_ref.shape, F32)
        accs_ref[...] = jnp.zeros(accs_ref.shape, F32)
        tq = lax.broadcasted_iota(jnp.int32, (Q_ROWS, PAGE), 0) & (DEC_SEQ - 1)
        key = lax.broadcasted_iota(jnp.int32, (Q_ROWS, PAGE), 1)
        _decode_diff(qd_ref, [nkd], [nvd], key <= tq, m_ref, l_ref, accd_ref)
        _decode_stick(qs_ref, [nks], [nvs], key < tq, u, r_ref, accs_ref)

    _decode_diff(qd_ref, pages[0::4], pages[1::4], None, m_ref, l_ref, accd_ref)

    @pl.when(jnp.max(r_ref[...]) > SB_CUTOFF)
    def _():
        _decode_stick(qs_ref, pages[2::4], pages[3::4], None, u, r_ref, accs_ref)

    @pl.when(s_id == pl.num_programs(1) - 1)
    def _():
        lam = _lambda(lq1, lk1, lq2, lk2)
        for g in range(DIFF_KV_HEADS):
            o = _diff_finish(accd_ref[2 * g], l_ref[2 * g], accd_ref[2 * g + 1], l_ref[2 * g + 1], lam, sg_ref[...])
            od_ref[g] = o.astype(BF16)
        for h in range(SB_KV_HEADS):
            os_ref[h] = accs_ref[h].astype(BF16)


def _decode(page_table, lams, subln, qd, qs, new_pages, caches):
    nb, n_pages = page_table.shape
    steps = n_pages // PAGES_PER_STEP
    zero2 = lambda b, s, pt: (0, 0)
    per_b4 = lambda b, s, pt: (b, 0, 0, 0)
    per_b3 = lambda b, s, pt: (b, 0, 0)

    def page_spec(p):
        return pl.BlockSpec((None, PAGE * CHUNKS, LANES),
                            lambda b, s, pt: (pt[b, n_pages - 1 - (s * PAGES_PER_STEP + p)], 0, 0))

    in_specs = ([pl.BlockSpec((1, HEAD_DIM), zero2)] * 4 + [pl.BlockSpec((1, DIFF_V_DIM), zero2)]
                + [pl.BlockSpec((None, 4, Q_ROWS, HEAD_DIM), per_b4)] * 2
                + [pl.BlockSpec((None, PAGE * CHUNKS, LANES), per_b3)] * 4)
    page_args = []
    for p in range(PAGES_PER_STEP):
        in_specs += [page_spec(p)] * 4
        page_args += list(caches)
    grid_spec = pltpu.PrefetchScalarGridSpec(
        num_scalar_prefetch=1,
        grid=(nb, steps),
        in_specs=in_specs,
        out_specs=[pl.BlockSpec((None, DIFF_KV_HEADS, Q_ROWS, DIFF_V_DIM), per_b4),
                   pl.BlockSpec((None, SB_KV_HEADS, Q_ROWS, HEAD_DIM), per_b4)],
        scratch_shapes=[
            pltpu.VMEM((4, Q_ROWS, LANES), F32), pltpu.VMEM((4, Q_ROWS, LANES), F32),
            pltpu.VMEM((4, Q_ROWS, DIFF_V_DIM), F32),
            pltpu.VMEM((SB_KV_HEADS, Q_ROWS, LANES), F32), pltpu.VMEM((SB_KV_HEADS, Q_ROWS, HEAD_DIM), F32),
        ],
    )
    return pl.pallas_call(
        _decode_kernel,
        grid_spec=grid_spec,
        out_shape=[jax.ShapeDtypeStruct((nb, DIFF_KV_HEADS, Q_ROWS, DIFF_V_DIM), BF16),
                   jax.ShapeDtypeStruct((nb, SB_KV_HEADS, Q_ROWS, HEAD_DIM), BF16)],
        compiler_params=_cparams(("arbitrary", "arbitrary")),
        name="decode_attn",
    )(page_table, *lams, subln, qd, qs, *new_pages, *page_args)


def _merge_kernel(od_ref, os_ref, gd_ref, gs_ref, wd_ref, ws_ref, o_ref):
    g_d = 1.0 / (1.0 + jnp.exp(-gd_ref[...]))
    g_s = 1.0 / (1.0 + jnp.exp(-gs_ref[...]))
    o_ref[...] = (g_d * _dot(od_ref[...], wd_ref[...]) + g_s * _dot(os_ref[...], ws_ref[...])).astype(BF16)


def _merge(od, os_, z32, wbd, wbs, tm):
    t = od.shape[0]
    tn = PROJ_TN
    gd0 = C_GATE // tn
    gs0 = (C_GATE + D_MODEL) // tn
    return pl.pallas_call(
        _merge_kernel,
        grid=(t // tm, D_MODEL // tn),
        in_specs=[
            pl.BlockSpec((tm, D_MODEL), lambda i, j: (i, 0)),
            pl.BlockSpec((tm, D_MODEL), lambda i, j: (i, 0)),
            pl.BlockSpec((tm, tn), lambda i, j: (i, gd0 + j)),
            pl.BlockSpec((tm, tn), lambda i, j: (i, gs0 + j)),
            pl.BlockSpec((D_MODEL, tn), lambda i, j: (0, j)),
            pl.BlockSpec((D_MODEL, tn), lambda i, j: (0, j)),
        ],
        out_specs=pl.BlockSpec((tm, tn), lambda i, j: (i, j)),
        out_shape=jax.ShapeDtypeStruct((t, D_MODEL), BF16),
        compiler_params=_cparams(("arbitrary", "arbitrary")),
        name="merge",
    )(od, os_, z32, z32, wbd, wbs)


def _out_kernel(x_ref, m_ref, w_ref, o_ref):
    o_ref[...] = x_ref[...] + _dot(m_ref[...], w_ref[...])


def _out_proj(x, merged, w_out, tm):
    t = x.shape[0]
    tn = PROJ_TN
    return pl.pallas_call(
        _out_kernel,
        grid=(t // tm, D_MODEL // tn),
        in_specs=[
            pl.BlockSpec((tm, tn), lambda i, j: (i, j)),
            pl.BlockSpec((tm, D_MODEL), lambda i, j: (i, 0)),
            pl.BlockSpec((D_MODEL, tn), lambda i, j: (0, j)),
        ],
        out_specs=pl.BlockSpec((tm, tn), lambda i, j: (i, j)),
        out_shape=jax.ShapeDtypeStruct((t, D_MODEL), F32),
        compiler_params=_cparams(("arbitrary", "arbitrary")),
        name="out_proj",
    )(x, merged, w_out)


ROW_CHUNKS = D_MODEL // LANES


def _store_row_major(ref, x):
    rows = x.shape[0]
    for c in range(ROW_CHUNKS):
        ref[pl.ds(c, rows, stride=ROW_CHUNKS), :] = x[:, c * LANES:(c + 1) * LANES]


def _load_row_major(ref, rows):
    return jnp.concatenate([ref[pl.ds(c, rows, stride=ROW_CHUNKS), :] for c in range(ROW_CHUNKS)], axis=1)


def _router_kernel(x_ref, g_ref, w_ref, b_ref, hn_ref, comb_ref, route_ref, counts_ref, carry_ref):
    hn32 = _rms(x_ref[...], g_ref[...])
    hn = hn32.astype(BF16)
    if hn_ref.dtype == BF16:
        hn_ref[...] = hn
    else:
        _store_row_major(hn_ref, hn32)
    logits = _dot(hn, w_ref[...]) + b_ref[...]
    gl = logits[:, :LANES]
    el = logits[:, LANES:]
    lane = lax.broadcasted_iota(jnp.int32, gl.shape, 1).astype(F32)
    far = float(LANES)

    def first_max(v):
        mx = jnp.max(v, axis=-1, keepdims=True)
        return mx, jnp.min(jnp.where(v == mx, lane, far), axis=-1, keepdims=True)

    gl = jnp.where(lane < N_GROUPS, gl, NEG_INF)
    gmax, grp = first_max(gl)
    p_grp = 1.0 / jnp.sum(jnp.exp(gl - gmax), axis=-1, keepdims=True)
    lo = grp * EXPERTS_PER_GROUP
    es = jnp.where((lane >= lo) & (lane < lo + EXPERTS_PER_GROUP), el, NEG_INF)
    v1, i1 = first_max(es)
    es2 = jnp.where(lane == i1, NEG_INF, es)
    v2, i2 = first_max(es2)
    e2 = jnp.exp(v2 - v1)
    w1 = 1.0 / (1.0 + e2)
    w2 = e2 / (1.0 + e2)
    ga = p_grp * w1
    gb = p_grp * w2
    oh_a = jnp.where(lane == i1, 1.0, 0.0)
    oh_b = jnp.where(lane == i2, 1.0, 0.0)
    comb_ref[...] = ga * oh_a + gb * oh_b

    @pl.when(pl.program_id(0) == 0)
    def _():
        carry_ref[...] = jnp.zeros(carry_ref.shape, F32)

    tm = x_ref.shape[0]
    oh = oh_a + oh_b
    before = _dot(_strict_lower(tm), oh.astype(BF16)) + carry_ref[...]
    rank_a = jnp.sum(before * oh_a, axis=-1, keepdims=True)
    rank_b = jnp.sum(before * oh_b, axis=-1, keepdims=True)
    carry_ref[...] = carry_ref[...] + jnp.sum(oh, axis=0, keepdims=True)
    counts_ref[...] = carry_ref[...]
    fields = (i1, i2, ga, gb, rank_a, rank_b)
    route_ref[...] = functools.reduce(jnp.add, [jnp.where(lane == float(k), f, 0.0) for k, f in enumerate(fields)])


def _router(x2, g, w_r, b_r, tm, routed):
    t = x2.shape[0]
    if routed:
        hn_spec = pl.BlockSpec((tm * ROW_CHUNKS, LANES), lambda i: (i, 0))
        hn_shape = jax.ShapeDtypeStruct((t * ROW_CHUNKS, LANES), F32)
    else:
        hn_spec = pl.BlockSpec((tm, D_MODEL), lambda i: (i, 0))
        hn_shape = jax.ShapeDtypeStruct((t, D_MODEL), BF16)
    return pl.pallas_call(
        _router_kernel,
        grid=(t // tm,),
        in_specs=[
            pl.BlockSpec((tm, D_MODEL), lambda i: (i, 0)),
            pl.BlockSpec((1, D_MODEL), lambda i: (0, 0)),
            pl.BlockSpec((D_MODEL, 2 * LANES), lambda i: (0, 0)),
            pl.BlockSpec((1, 2 * LANES), lambda i: (0, 0)),
        ],
        out_specs=[hn_spec, pl.BlockSpec((tm, LANES), lambda i: (i, 0)),
                   pl.BlockSpec((tm, LANES), lambda i: (i, 0)), pl.BlockSpec((1, LANES), lambda i: (0, 0))],
        out_shape=[hn_shape, jax.ShapeDtypeStruct((t, LANES), F32),
                   jax.ShapeDtypeStruct((t, LANES), F32), jax.ShapeDtypeStruct((1, LANES), F32)],
        scratch_shapes=[pltpu.VMEM((1, LANES), F32)],
        compiler_params=_cparams(("arbitrary",)),
        name="router",
    )(x2, g, w_r, b_r)


MOE_TM = 256


DMA_UNROLL = 8


def _moe_routed_kernel(te_ref, dprev_ref, dcur_ref, dnext_ref, wg_ref, wu_ref, wd_ref, hn_hbm, y_hbm,
                       xbuf, ybuf, gsem, ssem, *, n):
    j = pl.program_id(0)
    last = pl.num_programs(0) - 1
    cur = j % 2
    next_used = (j < last) & (te_ref[jnp.minimum(j + 1, last)] < N_EXPERTS)

    def rows_at(r):
        return pl.ds(pl.multiple_of(r * ROW_CHUNKS, ROW_CHUNKS), ROW_CHUNKS)

    def row_in(dref, b, r):
        d = dref[0, r]
        src = jnp.where(d < 0, 0, jnp.where(d >= n, d - n, d))
        return pltpu.make_async_copy(hn_hbm.at[rows_at(src), :], xbuf.at[b, rows_at(r), :], gsem.at[b])

    def row_out(dref, b, r):
        return pltpu.make_async_copy(ybuf.at[b, rows_at(r), :], y_hbm.at[rows_at(dref[0, r]), :], ssem.at[b])

    def each_row(fn):
        def body(r, c):
            fn(r)
            return c
        lax.fori_loop(0, MOE_TM, body, 0, unroll=DMA_UNROLL)

    def each_real_row(dref, fn):
        each_row(lambda r: pl.when(dref[0, r] >= 0)(lambda: fn(r)))

    @pl.when(te_ref[j] < N_EXPERTS)
    def _():
        @pl.when(j == 0)
        def _():
            each_row(lambda r: row_in(dcur_ref, cur, r).start())

        @pl.when(next_used)
        def _():
            each_row(lambda r: row_in(dnext_ref, 1 - cur, r).start())

        each_row(lambda r: row_in(dcur_ref, cur, r).wait())
        x = _load_row_major(xbuf.at[cur], MOE_TM).astype(BF16)
        gate = _dot(x, wg_ref[...].astype(BF16))
        hid = gate * (1.0 / (1.0 + jnp.exp(-gate))) * _dot(x, wu_ref[...].astype(BF16))
        _store_row_major(ybuf.at[cur], _dot(hid.astype(BF16), wd_ref[...].astype(BF16)))

        @pl.when(j > 0)
        def _():
            each_real_row(dprev_ref, lambda r: row_out(dprev_ref, 1 - cur, r).wait())

        each_real_row(dcur_ref, lambda r: row_out(dcur_ref, cur, r).start())

        @pl.when(jnp.logical_not(next_used))
        def _():
            each_real_row(dcur_ref, lambda r: row_out(dcur_ref, cur, r).wait())


def _moe_routed(hn, route, counts, wg, wu, wd):
    n = hn.shape[0] // ROW_CHUNKS
    n_tiles = (2 * n) // MOE_TM + N_EXPERTS
    p = n_tiles * MOE_TM
    experts = route[:, 0:2].astype(jnp.int32)
    ranks = route[:, 4:6].astype(jnp.int32)
    cnt = counts[0, :N_EXPERTS].astype(jnp.int32)
    padded = (cnt + MOE_TM - 1) // MOE_TM * MOE_TM
    ends = jnp.cumsum(padded)
    slot = ((ends - padded)[experts] + ranks).reshape(-1)
    tok = jnp.arange(n, dtype=jnp.int32)
    dst = jnp.full((p,), -1, jnp.int32).at[slot].set(jnp.stack([tok, tok + n], axis=1).reshape(-1))
    tile_expert = jnp.sum(jnp.arange(n_tiles, dtype=jnp.int32)[:, None] * MOE_TM >= ends[None, :], axis=1)
    tile_expert = tile_expert.astype(jnp.int32)

    def w_spec(shape):
        return pl.BlockSpec((None,) + shape, lambda j, te: (jnp.minimum(te[j], N_EXPERTS - 1), 0, 0))

    def dst_spec(shift):
        return pl.BlockSpec((None, 1, MOE_TM), lambda j, te: (jnp.clip(j + shift, 0, n_tiles - 1), 0, 0),
                            memory_space=pltpu.SMEM)

    grid_spec = pltpu.PrefetchScalarGridSpec(
        num_scalar_prefetch=1,
        grid=(n_tiles,),
        in_specs=[dst_spec(-1), dst_spec(0), dst_spec(1),
                  w_spec((D_MODEL, D_EXPERT)), w_spec((D_MODEL, D_EXPERT)), w_spec((D_EXPERT, D_MODEL)),
                  pl.BlockSpec(memory_space=pl.ANY)],
        out_specs=pl.BlockSpec(memory_space=pl.ANY),
        scratch_shapes=[pltpu.VMEM((2, MOE_TM * ROW_CHUNKS, LANES), F32),
                        pltpu.VMEM((2, MOE_TM * ROW_CHUNKS, LANES), F32),
                        pltpu.SemaphoreType.DMA((2,)), pltpu.SemaphoreType.DMA((2,))],
    )
    dst = dst.reshape(n_tiles, 1, MOE_TM)
    return pl.pallas_call(
        functools.partial(_moe_routed_kernel, n=n),
        grid_spec=grid_spec,
        out_shape=jax.ShapeDtypeStruct((2 * n * ROW_CHUNKS, LANES), F32),
        compiler_params=_cparams(("arbitrary",)),
        name="moe_routed",
    )(tile_expert, dst, dst, dst, wg, wu, wd, hn)


def _combine_kernel(x_ref, route_ref, ya_ref, yb_ref, g_ref, o_ref):
    tm = x_ref.shape[0]
    route = route_ref[...]
    lane = lax.broadcasted_iota(jnp.int32, route.shape, 1)
    ga = jnp.sum(jnp.where(lane == 2, route, 0.0), axis=-1, keepdims=True)
    gb = jnp.sum(jnp.where(lane == 3, route, 0.0), axis=-1, keepdims=True)
    moe = ga * _load_row_major(ya_ref, tm) + gb * _load_row_major(yb_ref, tm)
    o_ref[...] = _rms(x_ref[...] + moe, g_ref[...])


def _combine(x2, route, y2, gf, tm):
    t = x2.shape[0]
    nblk = t // tm
    return pl.pallas_call(
        _combine_kernel,
        grid=(nblk,),
        in_specs=[pl.BlockSpec((tm, D_MODEL), lambda i: (i, 0)),
                  pl.BlockSpec((tm, LANES), lambda i: (i, 0)),
                  pl.BlockSpec((tm * ROW_CHUNKS, LANES), lambda i: (i, 0)),
                  pl.BlockSpec((tm * ROW_CHUNKS, LANES), lambda i: (i + nblk, 0)),
                  pl.BlockSpec((1, D_MODEL), lambda i: (0, 0))],
        out_specs=pl.BlockSpec((tm, D_MODEL), lambda i: (i, 0)),
        out_shape=jax.ShapeDtypeStruct((t, D_MODEL), F32),
        compiler_params=_cparams(("arbitrary",)),
        name="moe_combine",
    )(x2, route, y2, y2, gf)


def _moe_kernel(hn_ref, comb_ref, x_ref, wg_ref, wu_ref, wd_ref, gf_ref, y_ref):
    e = pl.program_id(1)

    @pl.when(e == 0)
    def _():
        y_ref[...] = x_ref[...]

    hn = hn_ref[...]
    lane = lax.broadcasted_iota(jnp.int32, comb_ref.shape, 1)
    c = jnp.sum(jnp.where(lane == e, comb_ref[...], 0.0), axis=-1, keepdims=True)
    gate = _dot(hn, wg_ref[...].astype(BF16))
    hid = gate * (1.0 / (1.0 + jnp.exp(-gate))) * _dot(hn, wu_ref[...].astype(BF16))
    y_ref[...] += _dot((hid * c).astype(BF16), wd_ref[...].astype(BF16))

    @pl.when(e == pl.num_programs(1) - 1)
    def _():
        y_ref[...] = _rms(y_ref[...], gf_ref[...])


def _moe(hn, comb, x2, wg, wu, wd, gf, tm):
    t = hn.shape[0]
    return pl.pallas_call(
        _moe_kernel,
        grid=(t // tm, N_EXPERTS),
        in_specs=[
            pl.BlockSpec((tm, D_MODEL), lambda i, e: (i, 0)),
            pl.BlockSpec((tm, LANES), lambda i, e: (i, 0)),
            pl.BlockSpec((tm, D_MODEL), lambda i, e: (i, 0)),
            pl.BlockSpec((None, D_MODEL, D_EXPERT), lambda i, e: (e, 0, 0)),
            pl.BlockSpec((None, D_MODEL, D_EXPERT), lambda i, e: (e, 0, 0)),
            pl.BlockSpec((None, D_EXPERT, D_MODEL), lambda i, e: (e, 0, 0)),
            pl.BlockSpec((1, D_MODEL), lambda i, e: (0, 0)),
        ],
        out_specs=pl.BlockSpec((tm, D_MODEL), lambda i, e: (i, 0)),
        out_shape=jax.ShapeDtypeStruct((t, D_MODEL), F32),
        compiler_params=_cparams(("arbitrary", "arbitrary")),
        name="moe",
    )(hn, comb, x2, wg, wu, wd, gf)


def _post_attention(x, od, os_, z32, w, tm):
    merged = _merge(od, os_, z32, w["wbd"], w["wbs"], tm)
    x2 = _out_proj(x, merged, w["wout"], tm)
    if x.shape[0] // MOE_TM < N_EXPERTS:
        hn, comb, _, _ = _router(x2, w["g_ffn"], w["w_r"], w["b_r"], tm, False)
        return _moe(hn, comb, x2, w["wg"], w["wu"], w["wd"], w["g_final"], tm)
    hn, _, route, counts = _router(x2, w["g_ffn"], w["w_r"], w["b_r"], tm, True)
    y2 = _moe_routed(hn, route, counts, w["wg"], w["wu"], w["wd"])
    return _combine(x2, route, y2, w["g_final"], tm)


def _pad_rows(a, rows):
    return jnp.pad(a, ((0, rows - a.shape[0]),) + ((0, 0),) * (a.ndim - 1))


def kernel(x_prompt, x_sample, cache_k_diff, cache_v_diff, cache_k_sb, cache_v_sb, page_table, meta_tokens, norm_mix_g, w_in, lambda_q1, lambda_k1, lambda_q2, lambda_k2, subln_g, w_branch_diff, w_branch_sb, w_out, norm_ffn_g, w_router_group, b_router_group, w_router_expert, b_router_expert, w_expert_gate, w_expert_up, w_expert_down, norm_final_g):
    assert x_prompt.shape[0] == 1 and norm_mix_g.shape[0] == 1
    seq = x_prompt.shape[1]
    tp_real = N_META + seq
    tp = -(-tp_real // ROW_BLOCK) * ROW_BLOCK
    assert tp % ATT_BLOCK == 0
    nb, dec_seq, _ = x_sample.shape
    ts = nb * dec_seq
    n_phys = cache_k_diff.shape[1]
    past_len = page_table.shape[1] * PAGE
    assert dec_seq == DEC_SEQ and cache_k_diff.shape[2] == PAGE

    w_in_b = w_in[0].astype(BF16)
    w_r = jnp.zeros((D_MODEL, 2 * LANES), F32)
    w_r = w_r.at[:, :N_GROUPS].set(w_router_group[0]).at[:, LANES:LANES + N_EXPERTS].set(w_router_expert[0])
    b_r = jnp.zeros((1, 2 * LANES), F32)
    b_r = b_r.at[0, :N_GROUPS].set(b_router_group[0]).at[0, LANES:LANES + N_EXPERTS].set(b_router_expert[0])
    w = dict(
        wbd=w_branch_diff[0].astype(BF16), wbs=w_branch_sb[0].astype(BF16), wout=w_out[0].astype(BF16),
        g_ffn=norm_ffn_g, w_r=w_r.astype(BF16), b_r=b_r,
        wg=w_expert_gate[0], wu=w_expert_up[0], wd=w_expert_down[0],
        g_final=norm_final_g.reshape(1, D_MODEL),
    )
    lams = (lambda_q1, lambda_k1, lambda_q2, lambda_k2)

    half = HEAD_DIM // 2
    inv = ROPE_THETA ** (-jnp.arange(half, dtype=F32) * 2.0 / HEAD_DIM)
    inv = jnp.concatenate([inv, inv]).reshape(1, LANES)

    xp = _pad_rows(jnp.concatenate([meta_tokens.astype(F32), x_prompt[0]], axis=0), tp)
    pos_p = jnp.arange(tp, dtype=jnp.int32).astype(F32).reshape(tp, 1)
    cos_p, sin_p = _rope_tables(pos_p, inv, ROW_BLOCK)
    z32_p, zb_p = _project(xp, norm_mix_g, w_in_b, cos_p, sin_p, ROW_BLOCK)
    od_p = _diff_prompt(zb_p, lams, subln_g)
    os_p = _sb_prompt(zb_p)
    y_p = _post_attention(xp, od_p, os_p, z32_p, w, ROW_BLOCK)

    xs = x_sample.reshape(ts, D_MODEL)
    pos_s = jnp.tile(past_len + jnp.arange(dec_seq, dtype=jnp.int32), nb).astype(F32).reshape(ts, 1)
    cos_s, sin_s = _rope_tables(pos_s, inv, ts)
    z32_s, zb_s = _project(xs, norm_mix_g, w_in_b, cos_s, sin_s, ts)

    qd = zb_s[:, C_QD:C_KD].reshape(nb, dec_seq, DIFF_KV_HEADS, DIFF_REP, 2, HEAD_DIM)
    qd = qd.transpose(0, 2, 4, 3, 1, 5).reshape(nb, 4, Q_ROWS, HEAD_DIM)
    qs = zb_s[:, C_QS:C_KS].reshape(nb, dec_seq, SB_KV_HEADS, SB_REP, HEAD_DIM)
    qs = qs.transpose(0, 2, 3, 1, 4).reshape(nb, SB_KV_HEADS, Q_ROWS, HEAD_DIM)

    def new_page(a):
        return jnp.pad(a.reshape(nb, dec_seq * CHUNKS, LANES), ((0, 0), (0, (PAGE - dec_seq) * CHUNKS), (0, 0)))

    nvd = z32_s[:, C_VD:C_QS].reshape(nb, dec_seq, DIFF_KV_HEADS, 2, LANES).transpose(0, 1, 3, 2, 4)
    new_pages = (new_page(z32_s[:, C_KD:C_VD]), new_page(nvd),
                 new_page(z32_s[:, C_KS:C_VS]), new_page(z32_s[:, C_VS:C_GATE]))
    cvd = cache_v_diff[0].reshape(n_phys, PAGE, DIFF_KV_HEADS, 2, LANES).transpose(0, 1, 3, 2, 4)
    caches = (cache_k_diff[0].reshape(n_phys, PAGE * CHUNKS, LANES), cvd.reshape(n_phys, PAGE * CHUNKS, LANES),
              cache_k_sb[0].reshape(n_phys, PAGE * CHUNKS, LANES), cache_v_sb[0].reshape(n_phys, PAGE * CHUNKS, LANES))
    od_s, os_s = _decode(page_table, lams, subln_g, qd, qs, new_pages, caches)
    od_s = od_s.reshape(nb, DIFF_KV_HEADS, DIFF_REP, dec_seq, DIFF_V_DIM).transpose(0, 3, 1, 2, 4).reshape(ts, D_MODEL)
    os_s = os_s.reshape(nb, SB_KV_HEADS, SB_REP, dec_seq, HEAD_DIM).transpose(0, 3, 1, 2, 4).reshape(ts, D_MODEL)
    y_s = _post_attention(xs, od_s, os_s, z32_s, w, ts)

    def kv(z, rows, lead):
        return (z[:rows, C_KD:C_VD].reshape(lead + (DIFF_KV_HEADS, 2, HEAD_DIM)),
                z[:rows, C_VD:C_QS].reshape(lead + (DIFF_KV_HEADS, DIFF_V_DIM)),
                z[:rows, C_KS:C_VS].reshape(lead + (SB_KV_HEADS, HEAD_DIM)),
                z[:rows, C_VS:C_GATE].reshape(lead + (SB_KV_HEADS, HEAD_DIM)))

    y_prompt = y_p[N_META:tp_real].reshape(1, seq, D_MODEL)
    y_sample = y_s.reshape(nb, dec_seq, D_MODEL)
    return (y_prompt, y_sample) + kv(z32_p, tp_real, (1, 1, tp_real)) + kv(z32_s, ts, (1, nb, dec_seq))
```

```python
import functools
import math

import jax
import jax.numpy as jnp
import numpy as np
from jax import lax
from jax.experimental import pallas as pl
from jax.experimental.pallas import tpu as pltpu

F32 = jnp.float32
BF16 = jnp.bfloat16

D_MODEL = 2048
N_META = 16
HEAD_DIM = 128
DIFF_KV_HEADS = 2
DIFF_REP = 4
DIFF_V_DIM = 2 * HEAD_DIM
SB_KV_HEADS = 4
SB_REP = 4
ROPE_THETA = 10000.0
N_GROUPS = 4
EXPERTS_PER_GROUP = 8
N_EXPERTS = N_GROUPS * EXPERTS_PER_GROUP
D_EXPERT = 512
RMS_EPS = 1e-6
NEG_INF = -1e30
LAM_INIT = 0.8 - 0.6 * math.exp(-0.3 * 0)
SCALE = HEAD_DIM ** -0.5
LOG2E = math.log2(math.e)

C_QD, C_KD, C_VD, C_QS, C_KS, C_VS, C_GATE = 0, 2048, 2560, 3072, 5120, 5632, 6144
IN_COLS = 10240
ATTN_COLS = C_GATE

LANES = 128
PROJ_TN = 512
ROW_BLOCK = 768
ATT_BLOCK = 256
PAGES_PER_STEP = 16
VMEM_LIMIT = 56 * 1024 * 1024


def _cparams(sem):
    return pltpu.CompilerParams(dimension_semantics=sem, vmem_limit_bytes=VMEM_LIMIT)


def _dot(a, b):
    return jnp.dot(a, b, preferred_element_type=F32)


def _dot_nt(a, b):
    return lax.dot_general(a, b, (((1,), (1,)), ((), ())), preferred_element_type=F32)


def _rms(x, g):
    return x * lax.rsqrt(jnp.mean(x * x, axis=-1, keepdims=True) + RMS_EPS) * g


def _rope_table_kernel(pos_ref, inv_ref, cos_ref, sin_ref):
    ang = pos_ref[...] * inv_ref[...]
    lane = lax.broadcasted_iota(jnp.int32, ang.shape, 1)
    s = jnp.sin(ang)
    cos_ref[...] = jnp.cos(ang)
    sin_ref[...] = jnp.where(lane < HEAD_DIM // 2, -s, s)


def _rope_tables(pos, inv, tm):
    t = pos.shape[0]
    return pl.pallas_call(
        _rope_table_kernel,
        grid=(t // tm,),
        in_specs=[pl.BlockSpec((tm, 1), lambda i: (i, 0)), pl.BlockSpec((1, LANES), lambda i: (0, 0))],
        out_specs=[pl.BlockSpec((tm, LANES), lambda i: (i, 0))] * 2,
        out_shape=[jax.ShapeDtypeStruct((t, LANES), F32)] * 2,
        compiler_params=_cparams(("arbitrary",)),
        name="rope_tables",
    )(pos, inv)


N_ROPE_BLOCKS = C_VD // PROJ_TN
N_ATTN_BLOCKS = ATTN_COLS // PROJ_TN


def _proj_kernel(x_ref, g_ref, w_ref, cos_ref, sin_ref, z_ref, zb_ref, h_ref):
    j = pl.program_id(1)

    @pl.when(j == 0)
    def _():
        h_ref[...] = _rms(x_ref[...], g_ref[...]).astype(BF16)

    acc = _dot(h_ref[...], w_ref[...])

    @pl.when(j < N_ROPE_BLOCKS)
    def _():
        cos = cos_ref[...]
        sin = sin_ref[...]
        parts = []
        for c in range(PROJ_TN // HEAD_DIM):
            a = acc[:, c * HEAD_DIM:(c + 1) * HEAD_DIM]
            parts.append(a * cos + pltpu.roll(a, HEAD_DIM // 2, 1) * sin)
        r = jnp.concatenate(parts, axis=1)
        z_ref[...] = r
        zb_ref[...] = (r * jnp.where(j < C_KD // PROJ_TN, SCALE * LOG2E, 1.0)).astype(BF16)

    @pl.when((j >= N_ROPE_BLOCKS) & (j < N_ATTN_BLOCKS))
    def _():
        z_ref[...] = acc
        is_qs = (j >= C_QS // PROJ_TN) & (j < C_KS // PROJ_TN)
        zb_ref[...] = (acc * jnp.where(is_qs, SCALE, 1.0)).astype(BF16)

    @pl.when(j >= N_ATTN_BLOCKS)
    def _():
        z_ref[...] = acc


def _project(x, g, w_bf16, cos, sin, tm):
    t = x.shape[0]
    return pl.pallas_call(
        _proj_kernel,
        grid=(t // tm, IN_COLS // PROJ_TN),
        in_specs=[
            pl.BlockSpec((tm, D_MODEL), lambda i, j: (i, 0)),
            pl.BlockSpec((1, D_MODEL), lambda i, j: (0, 0)),
            pl.BlockSpec((D_MODEL, PROJ_TN), lambda i, j: (0, j)),
            pl.BlockSpec((tm, LANES), lambda i, j: (i, 0)),
            pl.BlockSpec((tm, LANES), lambda i, j: (i, 0)),
        ],
        out_specs=[
            pl.BlockSpec((tm, PROJ_TN), lambda i, j: (i, j)),
            pl.BlockSpec((tm, PROJ_TN), lambda i, j: (i, jnp.minimum(j, N_ATTN_BLOCKS - 1))),
        ],
        out_shape=[jax.ShapeDtypeStruct((t, IN_COLS), F32), jax.ShapeDtypeStruct((t, ATTN_COLS), BF16)],
        scratch_shapes=[pltpu.VMEM((tm, D_MODEL), BF16)],
        compiler_params=_cparams(("arbitrary", "arbitrary")),
        name="in_proj",
    )(x, g, w_bf16, cos, sin)


def _lambda(lq1, lk1, lq2, lk2):
    s1 = jnp.sum(lq1[...] * lk1[...], axis=-1, keepdims=True)
    s2 = jnp.sum(lq2[...] * lk2[...], axis=-1, keepdims=True)
    return jnp.exp(s1) - jnp.exp(s2) + LAM_INIT


def _diff_finish(acc1, l1, acc2, l2, lam, subln):
    w = acc1.shape[1]
    o = acc1 / _lanes(l1, w) - lam * (acc2 / _lanes(l2, w))
    o = o * lax.rsqrt(jnp.mean(o * o, axis=-1, keepdims=True) + RMS_EPS)
    return o * subln * (1.0 - LAM_INIT)


def _lanes(x, width):
    return x if width == LANES else jnp.concatenate([x] * (width // LANES), axis=1)


def _softmax_step(s, v, m_ref, l_ref, acc_ref, idx):
    m_old = m_ref[idx]
    m_new = jnp.maximum(m_old, jnp.max(s, axis=-1, keepdims=True))
    alpha = jnp.exp2(m_old - m_new)
    p = jnp.exp2(s - _lanes(m_new, s.shape[1]))
    l_ref[idx] = alpha * l_ref[idx] + jnp.sum(p, axis=-1, keepdims=True)
    acc_ref[idx] = _lanes(alpha, v.shape[1]) * acc_ref[idx] + _dot(p.astype(BF16), v)
    m_ref[idx] = m_new


def _strict_lower(n):
    r = lax.broadcasted_iota(jnp.int32, (n, n), 0)
    c = lax.broadcasted_iota(jnp.int32, (n, n), 1)
    return jnp.where(r > c, 1.0, 0.0).astype(BF16)


def _log_keep(z):
    return -(jnp.maximum(z, 0.0) + jnp.log(1.0 + jnp.exp(-jnp.abs(z))))


def _split_bf16(x):
    hi = x.astype(BF16)
    return hi, (x - hi.astype(F32)).astype(BF16)


SB_CUTOFF = -104.0


def _stick_step(z, v, u, keep, r_ref, acc_ref, idx):
    lk = _log_keep(z)
    lkm = lk if keep is None else jnp.where(keep, lk, 0.0)
    hi, lo = _split_bf16(lkm)
    suffix = _dot(hi, u) + _dot(lo, u)
    r = r_ref[idx]
    a = jnp.exp(z + lk + suffix + _lanes(r, z.shape[1]))
    if keep is not None:
        a = jnp.where(keep, a, 0.0)
    acc_ref[idx] = acc_ref[idx] + _dot(a.astype(BF16), v)
    r_ref[idx] = r + jnp.sum(lkm, axis=-1, keepdims=True)


def _diff_prompt_kernel(lq1, lk1, lq2, lk2, sg_ref, q_ref, k_ref, v_ref, o_ref, m_ref, l_ref, acc_ref):
    qi = pl.program_id(1)
    tb = ATT_BLOCK
    m_ref[...] = jnp.full(m_ref.shape, NEG_INF, F32)
    l_ref[...] = jnp.zeros(l_ref.shape, F32)
    acc_ref[...] = jnp.zeros(acc_ref.shape, F32)

    def block(kb, causal):
        start = pl.multiple_of(kb * tb, tb)
        kblk = k_ref[pl.ds(start, tb), :]
        vblk = v_ref[pl.ds(start, tb), :]
        for r in range(DIFF_REP):
            for m in range(2):
                c0 = (r * 2 + m) * HEAD_DIM
                s = _dot_nt(q_ref[:, c0:c0 + HEAD_DIM], kblk[:, m * HEAD_DIM:(m + 1) * HEAD_DIM])
                if causal is not None:
                    s = jnp.where(causal, s, NEG_INF)
                _softmax_step(s, vblk, m_ref, l_ref, acc_ref, r * 2 + m)

    def body(kb, carry):
        block(kb, None)
        return carry

    lax.fori_loop(0, qi, body, 0)
    row = lax.broadcasted_iota(jnp.int32, (tb, tb), 0)
    col = lax.broadcasted_iota(jnp.int32, (tb, tb), 1)
    block(qi, col <= row)

    lam = _lambda(lq1, lk1, lq2, lk2)
    for r in range(DIFF_REP):
        o = _diff_finish(acc_ref[2 * r], l_ref[2 * r], acc_ref[2 * r + 1], l_ref[2 * r + 1], lam, sg_ref[...])
        o_ref[:, r * DIFF_V_DIM:(r + 1) * DIFF_V_DIM] = o.astype(BF16)


def _lam_specs(nd):
    zero = (lambda *a: (0, 0))
    return [pl.BlockSpec((1, HEAD_DIM), zero)] * 4 + [pl.BlockSpec((1, DIFF_V_DIM), zero)]


def _diff_prompt(zb, lams, subln):
    t = zb.shape[0]
    tb = ATT_BLOCK
    qw = DIFF_REP * 2 * HEAD_DIM
    return pl.pallas_call(
        _diff_prompt_kernel,
        grid=(DIFF_KV_HEADS, t // tb),
        in_specs=_lam_specs(2) + [
            pl.BlockSpec((tb, qw), lambda g, i: (i, g)),
            pl.BlockSpec((t, 2 * HEAD_DIM), lambda g, i: (0, C_KD // (2 * HEAD_DIM) + g)),
            pl.BlockSpec((t, DIFF_V_DIM), lambda g, i: (0, C_VD // DIFF_V_DIM + g)),
        ],
        out_specs=pl.BlockSpec((tb, DIFF_REP * DIFF_V_DIM), lambda g, i: (i, g)),
        out_shape=jax.ShapeDtypeStruct((t, D_MODEL), BF16),
        scratch_shapes=[
            pltpu.VMEM((2 * DIFF_REP, tb, LANES), F32),
            pltpu.VMEM((2 * DIFF_REP, tb, LANES), F32),
            pltpu.VMEM((2 * DIFF_REP, tb, DIFF_V_DIM), F32),
        ],
        compiler_params=_cparams(("arbitrary", "arbitrary")),
        name="diff_prompt",
    )(*lams, subln, zb, zb, zb)


def _sb_prompt_kernel(q_ref, k_ref, v_ref, o_ref, r_ref, acc_ref):
    qi = pl.program_id(1)
    tb = ATT_BLOCK
    r_ref[...] = jnp.zeros(r_ref.shape, F32)
    acc_ref[...] = jnp.zeros(acc_ref.shape, F32)
    u = _strict_lower(tb)

    def block(kb, keep):
        start = pl.multiple_of(kb * tb, tb)
        kblk = k_ref[pl.ds(start, tb), :]
        vblk = v_ref[pl.ds(start, tb), :]
        for r in range(SB_REP):
            z = _dot_nt(q_ref[:, r * HEAD_DIM:(r + 1) * HEAD_DIM], kblk)
            _stick_step(z, vblk, u, keep, r_ref, acc_ref, r)

    row = lax.broadcasted_iota(jnp.int32, (tb, tb), 0)
    col = lax.broadcasted_iota(jnp.int32, (tb, tb), 1)
    block(qi, col < row)

    def live(c):
        return (c[0] >= 0) & (c[1] > SB_CUTOFF)

    def body(c):
        block(c[0], None)
        return c[0] - 1, jnp.max(r_ref[...])

    lax.while_loop(live, body, (qi - 1, jnp.max(r_ref[...])))
    for r in range(SB_REP):
        o_ref[:, r * HEAD_DIM:(r + 1) * HEAD_DIM] = acc_ref[r].astype(BF16)


def _sb_prompt(zb):
    t = zb.shape[0]
    tb = ATT_BLOCK
    qw = SB_REP * HEAD_DIM
    return pl.pallas_call(
        _sb_prompt_kernel,
        grid=(SB_KV_HEADS, t // tb),
        in_specs=[
            pl.BlockSpec((tb, qw), lambda h, i: (i, C_QS // qw + h)),
            pl.BlockSpec((t, HEAD_DIM), lambda h, i: (0, C_KS // HEAD_DIM + h)),
            pl.BlockSpec((t, HEAD_DIM), lambda h, i: (0, C_VS // HEAD_DIM + h)),
        ],
        out_specs=pl.BlockSpec((tb, qw), lambda h, i: (i, h)),
        out_shape=jax.ShapeDtypeStruct((t, D_MODEL), BF16),
        scratch_shapes=[pltpu.VMEM((SB_REP, tb, LANES), F32), pltpu.VMEM((SB_REP, tb, HEAD_DIM), F32)],
        compiler_params=_cparams(("arbitrary", "arbitrary")),
        name="sb_prompt",
    )(zb, zb, zb)


PAGE = 128
CHUNKS = 4
Q_ROWS = 16
DEC_SEQ = 4


def _page_chunk(ref, c):
    return ref[pl.ds(c, PAGE, stride=CHUNKS), :].astype(BF16)


def _decode_diff(qd_ref, kds, vds, causal, m_ref, l_ref, acc_ref):
    n = len(kds)
    for g in range(DIFF_KV_HEADS):
        v = [[_page_chunk(vd, c * 2 + g) for c in range(2)] for vd in vds]
        for m in range(2):
            idx = g * 2 + m
            q = qd_ref[idx]
            s = [_dot_nt(q, _page_chunk(kd, idx)) for kd in kds]
            if causal is not None:
                s = [jnp.where(causal, x, NEG_INF) for x in s]
            m_old = m_ref[idx]
            m_new = jnp.maximum(m_old, jnp.max(functools.reduce(jnp.maximum, s), axis=-1, keepdims=True))
            alpha = jnp.exp2(m_old - m_new)
            p = [jnp.exp2(x - m_new) for x in s]
            l_ref[idx] = alpha * l_ref[idx] + jnp.sum(functools.reduce(jnp.add, p), axis=-1, keepdims=True)
            pb = [x.astype(BF16) for x in p]
            pv = [functools.reduce(jnp.add, [_dot(pb[i], v[i][c]) for i in range(n)]) for c in range(2)]
            acc_ref[idx] = _lanes(alpha, DIFF_V_DIM) * acc_ref[idx] + jnp.concatenate(pv, axis=1)
            m_ref[idx] = m_new


def _decode_stick(qs_ref, kss, vss, strict, u, r_ref, acc_ref):
    n = len(kss)
    rows = n * Q_ROWS
    for h in range(SB_KV_HEADS):
        q = qs_ref[h]
        z = jnp.concatenate([_dot_nt(q, _page_chunk(ks, h)) for ks in kss], axis=0)
        lk = _log_keep(z)
        keep = None if strict is None else jnp.concatenate([strict] * n, axis=0)
        lkm = lk if keep is None else jnp.where(keep, lk, 0.0)
        hi, lo = _split_bf16(lkm)
        s2 = _dot(jnp.concatenate([hi, lo], axis=0), u)
        suffix = s2[:rows] + s2[rows:]
        tot = jnp.sum(lkm, axis=-1, keepdims=True)
        r = r_ref[h]
        offs = []
        for i in range(n):
            offs.append(r)
            r = r + tot[i * Q_ROWS:(i + 1) * Q_ROWS]
        a = jnp.exp(z + lk + suffix + jnp.concatenate(offs, axis=0))
        if keep is not None:
            a = jnp.where(keep, a, 0.0)
        ab = a.astype(BF16)
        pv = [_dot(ab[i * Q_ROWS:(i + 1) * Q_ROWS], _page_chunk(vss[i], h)) for i in range(n)]
        acc_ref[h] = acc_ref[h] + functools.reduce(jnp.add, pv)
        r_ref[h] = r


def _decode_kernel(pt_ref, lq1, lk1, lq2, lk2, sg_ref, qd_ref, qs_ref, nkd, nvd, nks, nvs, *rest):
    n_in = 2 * PAGES_PER_STEP
    pages = rest[:n_in]
    ks_hbm, vs_hbm, od_ref, os_ref = rest[n_in:n_in + 4]
    m_ref, l_ref, accd_ref, r_ref, accs_ref, ks_buf, vs_buf, sb_sem = rest[n_in + 4:]
    b_id = pl.program_id(0)
    s_id = pl.program_id(1)
    u = _strict_lower(PAGE)

    @pl.when(s_id == 0)
    def _():
        m_ref[...] = jnp.full(m_ref.shape, NEG_INF, F32)
        l_ref[...] = jnp.zeros(l_ref.shape, F32)
        accd_ref[...] = jnp.zeros(accd_ref.shape, F32)
        r_ref[...] = jnp.zeros(r_ref.shape, F32)
        accs_ref[...] = jnp.zeros(accs_ref.shape, F32)
        tq = lax.broadcasted_iota(jnp.int32, (Q_ROWS, PAGE), 0) & (DEC_SEQ - 1)
        key = lax.broadcasted_iota(jnp.int32, (Q_ROWS, PAGE), 1)
        _decode_diff(qd_ref, [nkd], [nvd], key <= tq, m_ref, l_ref, accd_ref)
        _decode_stick(qs_ref, [nks], [nvs], key < tq, u, r_ref, accs_ref)

    _decode_diff(qd_ref, pages[0::2], pages[1::2], None, m_ref, l_ref, accd_ref)

    @pl.when(jnp.max(r_ref[...]) > SB_CUTOFF)
    def _():
        n_pages = pl.num_programs(1) * PAGES_PER_STEP

        def copies(p):
            page = pt_ref[b_id, n_pages - 1 - (s_id * PAGES_PER_STEP + p)]
            return (pltpu.make_async_copy(ks_hbm.at[page], ks_buf.at[p], sb_sem),
                    pltpu.make_async_copy(vs_hbm.at[page], vs_buf.at[p], sb_sem))

        for p in range(PAGES_PER_STEP):
            for c in copies(p):
                c.start()
        for p in range(PAGES_PER_STEP):
            for c in copies(p):
                c.wait()
        bufs = range(PAGES_PER_STEP)
        _decode_stick(qs_ref, [ks_buf.at[p] for p in bufs], [vs_buf.at[p] for p in bufs], None, u, r_ref, accs_ref)

    @pl.when(s_id == pl.num_programs(1) - 1)
    def _():
        lam = _lambda(lq1, lk1, lq2, lk2)
        for g in range(DIFF_KV_HEADS):
            o = _diff_finish(accd_ref[2 * g], l_ref[2 * g], accd_ref[2 * g + 1], l_ref[2 * g + 1], lam, sg_ref[...])
            od_ref[g] = o.astype(BF16)
        for h in range(SB_KV_HEADS):
            os_ref[h] = accs_ref[h].astype(BF16)


def _decode(page_table, lams, subln, qd, qs, new_pages, caches):
    nb, n_pages = page_table.shape
    steps = n_pages // PAGES_PER_STEP
    zero2 = lambda b, s, pt: (0, 0)
    per_b4 = lambda b, s, pt: (b, 0, 0, 0)
    per_b3 = lambda b, s, pt: (b, 0, 0)

    def page_spec(p):
        return pl.BlockSpec((None, PAGE * CHUNKS, LANES),
                            lambda b, s, pt: (pt[b, n_pages - 1 - (s * PAGES_PER_STEP + p)], 0, 0))

    in_specs = ([pl.BlockSpec((1, HEAD_DIM), zero2)] * 4 + [pl.BlockSpec((1, DIFF_V_DIM), zero2)]
                + [pl.BlockSpec((None, 4, Q_ROWS, HEAD_DIM), per_b4)] * 2
                + [pl.BlockSpec((None, PAGE * CHUNKS, LANES), per_b3)] * 4)
    page_args = []
    for p in range(PAGES_PER_STEP):
        in_specs += [page_spec(p)] * 2
        page_args += list(caches[:2])
    in_specs += [pl.BlockSpec(memory_space=pl.ANY)] * 2
    page_args += list(caches[2:])
    grid_spec = pltpu.PrefetchScalarGridSpec(
        num_scalar_prefetch=1,
        grid=(nb, steps),
        in_specs=in_specs,
        out_specs=[pl.BlockSpec((None, DIFF_KV_HEADS, Q_ROWS, DIFF_V_DIM), per_b4),
                   pl.BlockSpec((None, SB_KV_HEADS, Q_ROWS, HEAD_DIM), per_b4)],
        scratch_shapes=[
            pltpu.VMEM((4, Q_ROWS, LANES), F32), pltpu.VMEM((4, Q_ROWS, LANES), F32),
            pltpu.VMEM((4, Q_ROWS, DIFF_V_DIM), F32),
            pltpu.VMEM((SB_KV_HEADS, Q_ROWS, LANES), F32), pltpu.VMEM((SB_KV_HEADS, Q_ROWS, HEAD_DIM), F32),
            pltpu.VMEM((PAGES_PER_STEP, PAGE * CHUNKS, LANES), F32),
            pltpu.VMEM((PAGES_PER_STEP, PAGE * CHUNKS, LANES), F32),
            pltpu.SemaphoreType.DMA(()),
        ],
    )
    return pl.pallas_call(
        _decode_kernel,
        grid_spec=grid_spec,
        out_shape=[jax.ShapeDtypeStruct((nb, DIFF_KV_HEADS, Q_ROWS, DIFF_V_DIM), BF16),
                   jax.ShapeDtypeStruct((nb, SB_KV_HEADS, Q_ROWS, HEAD_DIM), BF16)],
        compiler_params=_cparams(("arbitrary", "arbitrary")),
        name="decode_attn",
    )(page_table, *lams, subln, qd, qs, *new_pages, *page_args)


def _merge_kernel(od_ref, os_ref, gd_ref, gs_ref, wd_ref, ws_ref, o_ref):
    g_d = 1.0 / (1.0 + jnp.exp(-gd_ref[...]))
    g_s = 1.0 / (1.0 + jnp.exp(-gs_ref[...]))
    o_ref[...] = (g_d * _dot(od_ref[...], wd_ref[...]) + g_s * _dot(os_ref[...], ws_ref[...])).astype(BF16)


def _merge(od, os_, z32, wbd, wbs, tm):
    t = od.shape[0]
    tn = PROJ_TN
    gd0 = C_GATE // tn
    gs0 = (C_GATE + D_MODEL) // tn
    return pl.pallas_call(
        _merge_kernel,
        grid=(t // tm, D_MODEL // tn),
        in_specs=[
            pl.BlockSpec((tm, D_MODEL), lambda i, j: (i, 0)),
            pl.BlockSpec((tm, D_MODEL), lambda i, j: (i, 0)),
            pl.BlockSpec((tm, tn), lambda i, j: (i, gd0 + j)),
            pl.BlockSpec((tm, tn), lambda i, j: (i, gs0 + j)),
            pl.BlockSpec((D_MODEL, tn), lambda i, j: (0, j)),
            pl.BlockSpec((D_MODEL, tn), lambda i, j: (0, j)),
        ],
        out_specs=pl.BlockSpec((tm, tn), lambda i, j: (i, j)),
        out_shape=jax.ShapeDtypeStruct((t, D_MODEL), BF16),
        compiler_params=_cparams(("arbitrary", "arbitrary")),
        name="merge",
    )(od, os_, z32, z32, wbd, wbs)


def _out_kernel(x_ref, m_ref, w_ref, o_ref):
    o_ref[...] = x_ref[...] + _dot(m_ref[...], w_ref[...])


def _out_proj(x, merged, w_out, tm):
    t = x.shape[0]
    tn = PROJ_TN
    return pl.pallas_call(
        _out_kernel,
        grid=(t // tm, D_MODEL // tn),
        in_specs=[
            pl.BlockSpec((tm, tn), lambda i, j: (i, j)),
            pl.BlockSpec((tm, D_MODEL), lambda i, j: (i, 0)),
            pl.BlockSpec((D_MODEL, tn), lambda i, j: (0, j)),
        ],
        out_specs=pl.BlockSpec((tm, tn), lambda i, j: (i, j)),
        out_shape=jax.ShapeDtypeStruct((t, D_MODEL), F32),
        compiler_params=_cparams(("arbitrary", "arbitrary")),
        name="out_proj",
    )(x, merged, w_out)


ROW_CHUNKS = D_MODEL // LANES


def _store_row_major(ref, x):
    rows = x.shape[0]
    for c in range(ROW_CHUNKS):
        ref[pl.ds(c, rows, stride=ROW_CHUNKS), :] = x[:, c * LANES:(c + 1) * LANES]


def _load_row_major(ref, rows):
    return jnp.concatenate([ref[pl.ds(c, rows, stride=ROW_CHUNKS), :] for c in range(ROW_CHUNKS)], axis=1)


def _router_kernel(x_ref, g_ref, w_ref, b_ref, hn_ref, comb_ref, route_ref, counts_ref, carry_ref):
    hn32 = _rms(x_ref[...], g_ref[...])
    hn = hn32.astype(BF16)
    if hn_ref.dtype == BF16:
        hn_ref[...] = hn
    else:
        _store_row_major(hn_ref, hn32)
    logits = _dot(hn, w_ref[...]) + b_ref[...]
    gl = logits[:, :LANES]
    el = logits[:, LANES:]
    lane = lax.broadcasted_iota(jnp.int32, gl.shape, 1).astype(F32)
    far = float(LANES)

    def first_max(v):
        mx = jnp.max(v, axis=-1, keepdims=True)
        return mx, jnp.min(jnp.where(v == mx, lane, far), axis=-1, keepdims=True)

    gl = jnp.where(lane < N_GROUPS, gl, NEG_INF)
    gmax, grp = first_max(gl)
    p_grp = 1.0 / jnp.sum(jnp.exp(gl - gmax), axis=-1, keepdims=True)
    lo = grp * EXPERTS_PER_GROUP
    es = jnp.where((lane >= lo) & (lane < lo + EXPERTS_PER_GROUP), el, NEG_INF)
    v1, i1 = first_max(es)
    es2 = jnp.where(lane == i1, NEG_INF, es)
    v2, i2 = first_max(es2)
    e2 = jnp.exp(v2 - v1)
    w1 = 1.0 / (1.0 + e2)
    w2 = e2 / (1.0 + e2)
    ga = p_grp * w1
    gb = p_grp * w2
    oh_a = jnp.where(lane == i1, 1.0, 0.0)
    oh_b = jnp.where(lane == i2, 1.0, 0.0)
    comb_ref[...] = ga * oh_a + gb * oh_b

    @pl.when(pl.program_id(0) == 0)
    def _():
        carry_ref[...] = jnp.zeros(carry_ref.shape, F32)

    tm = x_ref.shape[0]
    oh = oh_a + oh_b
    before = _dot(_strict_lower(tm), oh.astype(BF16)) + carry_ref[...]
    rank_a = jnp.sum(before * oh_a, axis=-1, keepdims=True)
    rank_b = jnp.sum(before * oh_b, axis=-1, keepdims=True)
    carry_ref[...] = carry_ref[...] + jnp.sum(oh, axis=0, keepdims=True)
    counts_ref[...] = carry_ref[...]
    fields = (i1, i2, ga, gb, rank_a, rank_b)
    route_ref[...] = functools.reduce(jnp.add, [jnp.where(lane == float(k), f, 0.0) for k, f in enumerate(fields)])


def _router(x2, g, w_r, b_r, tm, routed):
    t = x2.shape[0]
    if routed:
        hn_spec = pl.BlockSpec((tm * ROW_CHUNKS, LANES), lambda i: (i, 0))
        hn_shape = jax.ShapeDtypeStruct((t * ROW_CHUNKS, LANES), F32)
    else:
        hn_spec = pl.BlockSpec((tm, D_MODEL), lambda i: (i, 0))
        hn_shape = jax.ShapeDtypeStruct((t, D_MODEL), BF16)
    return pl.pallas_call(
        _router_kernel,
        grid=(t // tm,),
        in_specs=[
            pl.BlockSpec((tm, D_MODEL), lambda i: (i, 0)),
            pl.BlockSpec((1, D_MODEL), lambda i: (0, 0)),
            pl.BlockSpec((D_MODEL, 2 * LANES), lambda i: (0, 0)),
            pl.BlockSpec((1, 2 * LANES), lambda i: (0, 0)),
        ],
        out_specs=[hn_spec, pl.BlockSpec((tm, LANES), lambda i: (i, 0)),
                   pl.BlockSpec((tm, LANES), lambda i: (i, 0)), pl.BlockSpec((1, LANES), lambda i: (0, 0))],
        out_shape=[hn_shape, jax.ShapeDtypeStruct((t, LANES), F32),
                   jax.ShapeDtypeStruct((t, LANES), F32), jax.ShapeDtypeStruct((1, LANES), F32)],
        scratch_shapes=[pltpu.VMEM((1, LANES), F32)],
        compiler_params=_cparams(("arbitrary",)),
        name="router",
    )(x2, g, w_r, b_r)


MOE_TM = 256


DMA_UNROLL = 8


def _moe_routed_kernel(te_ref, dprev_ref, dcur_ref, dnext_ref, wg_ref, wu_ref, wd_ref, hn_hbm, y_hbm,
                       xbuf, ybuf, gsem, ssem, *, n):
    j = pl.program_id(0)
    last = pl.num_programs(0) - 1
    cur = j % 2
    next_used = (j < last) & (te_ref[jnp.minimum(j + 1, last)] < N_EXPERTS)

    def rows_at(r):
        return pl.ds(pl.multiple_of(r * ROW_CHUNKS, ROW_CHUNKS), ROW_CHUNKS)

    def row_in(dref, b, r):
        d = dref[0, r]
        src = jnp.where(d < 0, 0, jnp.where(d >= n, d - n, d))
        return pltpu.make_async_copy(hn_hbm.at[rows_at(src), :], xbuf.at[b, rows_at(r), :], gsem.at[b])

    def row_out(dref, b, r):
        return pltpu.make_async_copy(ybuf.at[b, rows_at(r), :], y_hbm.at[rows_at(dref[0, r]), :], ssem.at[b])

    def each_row(fn):
        def body(r, c):
            fn(r)
            return c
        lax.fori_loop(0, MOE_TM, body, 0, unroll=DMA_UNROLL)

    def each_real_row(dref, fn):
        each_row(lambda r: pl.when(dref[0, r] >= 0)(lambda: fn(r)))

    @pl.when(te_ref[j] < N_EXPERTS)
    def _():
        @pl.when(j == 0)
        def _():
            each_row(lambda r: row_in(dcur_ref, cur, r).start())

        @pl.when(next_used)
        def _():
            each_row(lambda r: row_in(dnext_ref, 1 - cur, r).start())

        each_row(lambda r: row_in(dcur_ref, cur, r).wait())
        x = _load_row_major(xbuf.at[cur], MOE_TM).astype(BF16)
        gate = _dot(x, wg_ref[...].astype(BF16))
        hid = gate * (1.0 / (1.0 + jnp.exp(-gate))) * _dot(x, wu_ref[...].astype(BF16))
        _store_row_major(ybuf.at[cur], _dot(hid.astype(BF16), wd_ref[...].astype(BF16)))

        @pl.when(j > 0)
        def _():
            each_real_row(dprev_ref, lambda r: row_out(dprev_ref, 1 - cur, r).wait())

        each_real_row(dcur_ref, lambda r: row_out(dcur_ref, cur, r).start())

        @pl.when(jnp.logical_not(next_used))
        def _():
            each_real_row(dcur_ref, lambda r: row_out(dcur_ref, cur, r).wait())


def _moe_routed(hn, route, counts, wg, wu, wd):
    n = hn.shape[0] // ROW_CHUNKS
    n_tiles = (2 * n) // MOE_TM + N_EXPERTS
    p = n_tiles * MOE_TM
    experts = route[:, 0:2].astype(jnp.int32)
    ranks = route[:, 4:6].astype(jnp.int32)
    cnt = counts[0, :N_EXPERTS].astype(jnp.int32)
    padded = (cnt + MOE_TM - 1) // MOE_TM * MOE_TM
    ends = jnp.cumsum(padded)
    slot = ((ends - padded)[experts] + ranks).reshape(-1)
    tok = jnp.arange(n, dtype=jnp.int32)
    dst = jnp.full((p,), -1, jnp.int32).at[slot].set(jnp.stack([tok, tok + n], axis=1).reshape(-1))
    tile_expert = jnp.sum(jnp.arange(n_tiles, dtype=jnp.int32)[:, None] * MOE_TM >= ends[None, :], axis=1)
    tile_expert = tile_expert.astype(jnp.int32)

    def w_spec(shape):
        return pl.BlockSpec((None,) + shape, lambda j, te: (jnp.minimum(te[j], N_EXPERTS - 1), 0, 0))

    def dst_spec(shift):
        return pl.BlockSpec((None, 1, MOE_TM), lambda j, te: (jnp.clip(j + shift, 0, n_tiles - 1), 0, 0),
                            memory_space=pltpu.SMEM)

    grid_spec = pltpu.PrefetchScalarGridSpec(
        num_scalar_prefetch=1,
        grid=(n_tiles,),
        in_specs=[dst_spec(-1), dst_spec(0), dst_spec(1),
                  w_spec((D_MODEL, D_EXPERT)), w_spec((D_MODEL, D_EXPERT)), w_spec((D_EXPERT, D_MODEL)),
                  pl.BlockSpec(memory_space=pl.ANY)],
        out_specs=pl.BlockSpec(memory_space=pl.ANY),
        scratch_shapes=[pltpu.VMEM((2, MOE_TM * ROW_CHUNKS, LANES), F32),
                        pltpu.VMEM((2, MOE_TM * ROW_CHUNKS, LANES), F32),
                        pltpu.SemaphoreType.DMA((2,)), pltpu.SemaphoreType.DMA((2,))],
    )
    dst = dst.reshape(n_tiles, 1, MOE_TM)
    return pl.pallas_call(
        functools.partial(_moe_routed_kernel, n=n),
        grid_spec=grid_spec,
        out_shape=jax.ShapeDtypeStruct((2 * n * ROW_CHUNKS, LANES), F32),
        compiler_params=_cparams(("arbitrary",)),
        name="moe_routed",
    )(tile_expert, dst, dst, dst, wg, wu, wd, hn)


def _combine_kernel(x_ref, route_ref, ya_ref, yb_ref, g_ref, o_ref):
    tm = x_ref.shape[0]
    route = route_ref[...]
    lane = lax.broadcasted_iota(jnp.int32, route.shape, 1)
    ga = jnp.sum(jnp.where(lane == 2, route, 0.0), axis=-1, keepdims=True)
    gb = jnp.sum(jnp.where(lane == 3, route, 0.0), axis=-1, keepdims=True)
    moe = ga * _load_row_major(ya_ref, tm) + gb * _load_row_major(yb_ref, tm)
    o_ref[...] = _rms(x_ref[...] + moe, g_ref[...])


def _combine(x2, route, y2, gf, tm):
    t = x2.shape[0]
    nblk = t // tm
    return pl.pallas_call(
        _combine_kernel,
        grid=(nblk,),
        in_specs=[pl.BlockSpec((tm, D_MODEL), lambda i: (i, 0)),
                  pl.BlockSpec((tm, LANES), lambda i: (i, 0)),
                  pl.BlockSpec((tm * ROW_CHUNKS, LANES), lambda i: (i, 0)),
                  pl.BlockSpec((tm * ROW_CHUNKS, LANES), lambda i: (i + nblk, 0)),
                  pl.BlockSpec((1, D_MODEL), lambda i: (0, 0))],
        out_specs=pl.BlockSpec((tm, D_MODEL), lambda i: (i, 0)),
        out_shape=jax.ShapeDtypeStruct((t, D_MODEL), F32),
        compiler_params=_cparams(("arbitrary",)),
        name="moe_combine",
    )(x2, route, y2, y2, gf)


def _moe_kernel(hn_ref, comb_ref, x_ref, wg_ref, wu_ref, wd_ref, gf_ref, y_ref):
    e = pl.program_id(1)

    @pl.when(e == 0)
    def _():
        y_ref[...] = x_ref[...]

    hn = hn_ref[...]
    lane = lax.broadcasted_iota(jnp.int32, comb_ref.shape, 1)
    c = jnp.sum(jnp.where(lane == e, comb_ref[...], 0.0), axis=-1, keepdims=True)
    gate = _dot(hn, wg_ref[...].astype(BF16))
    hid = gate * (1.0 / (1.0 + jnp.exp(-gate))) * _dot(hn, wu_ref[...].astype(BF16))
    y_ref[...] += _dot((hid * c).astype(BF16), wd_ref[...].astype(BF16))

    @pl.when(e == pl.num_programs(1) - 1)
    def _():
        y_ref[...] = _rms(y_ref[...], gf_ref[...])


def _moe(hn, comb, x2, wg, wu, wd, gf, tm):
    t = hn.shape[0]
    return pl.pallas_call(
        _moe_kernel,
        grid=(t // tm, N_EXPERTS),
        in_specs=[
            pl.BlockSpec((tm, D_MODEL), lambda i, e: (i, 0)),
            pl.BlockSpec((tm, LANES), lambda i, e: (i, 0)),
            pl.BlockSpec((tm, D_MODEL), lambda i, e: (i, 0)),
            pl.BlockSpec((None, D_MODEL, D_EXPERT), lambda i, e: (e, 0, 0)),
            pl.BlockSpec((None, D_MODEL, D_EXPERT), lambda i, e: (e, 0, 0)),
            pl.BlockSpec((None, D_EXPERT, D_MODEL), lambda i, e: (e, 0, 0)),
            pl.BlockSpec((1, D_MODEL), lambda i, e: (0, 0)),
        ],
        out_specs=pl.BlockSpec((tm, D_MODEL), lambda i, e: (i, 0)),
        out_shape=jax.ShapeDtypeStruct((t, D_MODEL), F32),
        compiler_params=_cparams(("arbitrary", "arbitrary")),
        name="moe",
    )(hn, comb, x2, wg, wu, wd, gf)


def _post_attention(x, od, os_, z32, w, tm):
    merged = _merge(od, os_, z32, w["wbd"], w["wbs"], tm)
    x2 = _out_proj(x, merged, w["wout"], tm)
    if x.shape[0] // MOE_TM < N_EXPERTS:
        hn, comb, _, _ = _router(x2, w["g_ffn"], w["w_r"], w["b_r"], tm, False)
        return _moe(hn, comb, x2, w["wg"], w["wu"], w["wd"], w["g_final"], tm)
    hn, _, route, counts = _router(x2, w["g_ffn"], w["w_r"], w["b_r"], tm, True)
    y2 = _moe_routed(hn, route, counts, w["wg"], w["wu"], w["wd"])
    return _combine(x2, route, y2, w["g_final"], tm)


def _pad_rows(a, rows):
    return jnp.pad(a, ((0, rows - a.shape[0]),) + ((0, 0),) * (a.ndim - 1))


def kernel(x_prompt, x_sample, cache_k_diff, cache_v_diff, cache_k_sb, cache_v_sb, page_table, meta_tokens, norm_mix_g, w_in, lambda_q1, lambda_k1, lambda_q2, lambda_k2, subln_g, w_branch_diff, w_branch_sb, w_out, norm_ffn_g, w_router_group, b_router_group, w_router_expert, b_router_expert, w_expert_gate, w_expert_up, w_expert_down, norm_final_g):
    assert x_prompt.shape[0] == 1 and norm_mix_g.shape[0] == 1
    seq = x_prompt.shape[1]
    tp_real = N_META + seq
    tp = -(-tp_real // ROW_BLOCK) * ROW_BLOCK
    assert tp % ATT_BLOCK == 0
    nb, dec_seq, _ = x_sample.shape
    ts = nb * dec_seq
    n_phys = cache_k_diff.shape[1]
    past_len = page_table.shape[1] * PAGE
    assert dec_seq == DEC_SEQ and cache_k_diff.shape[2] == PAGE

    w_in_b = w_in[0].astype(BF16)
    w_r = jnp.zeros((D_MODEL, 2 * LANES), F32)
    w_r = w_r.at[:, :N_GROUPS].set(w_router_group[0]).at[:, LANES:LANES + N_EXPERTS].set(w_router_expert[0])
    b_r = jnp.zeros((1, 2 * LANES), F32)
    b_r = b_r.at[0, :N_GROUPS].set(b_router_group[0]).at[0, LANES:LANES + N_EXPERTS].set(b_router_expert[0])
    w = dict(
        wbd=w_branch_diff[0].astype(BF16), wbs=w_branch_sb[0].astype(BF16), wout=w_out[0].astype(BF16),
        g_ffn=norm_ffn_g, w_r=w_r.astype(BF16), b_r=b_r,
        wg=w_expert_gate[0], wu=w_expert_up[0], wd=w_expert_down[0],
        g_final=norm_final_g.reshape(1, D_MODEL),
    )
    lams = (lambda_q1, lambda_k1, lambda_q2, lambda_k2)

    half = HEAD_DIM // 2
    inv = ROPE_THETA ** (-jnp.arange(half, dtype=F32) * 2.0 / HEAD_DIM)
    inv = jnp.concatenate([inv, inv]).reshape(1, LANES)

    xp = _pad_rows(jnp.concatenate([meta_tokens.astype(F32), x_prompt[0]], axis=0), tp)
    pos_p = jnp.arange(tp, dtype=jnp.int32).astype(F32).reshape(tp, 1)
    cos_p, sin_p = _rope_tables(pos_p, inv, ROW_BLOCK)
    z32_p, zb_p = _project(xp, norm_mix_g, w_in_b, cos_p, sin_p, ROW_BLOCK)
    od_p = _diff_prompt(zb_p, lams, subln_g)
    os_p = _sb_prompt(zb_p)
    y_p = _post_attention(xp, od_p, os_p, z32_p, w, ROW_BLOCK)

    xs = x_sample.reshape(ts, D_MODEL)
    pos_s = jnp.tile(past_len + jnp.arange(dec_seq, dtype=jnp.int32), nb).astype(F32).reshape(ts, 1)
    cos_s, sin_s = _rope_tables(pos_s, inv, ts)
    z32_s, zb_s = _project(xs, norm_mix_g, w_in_b, cos_s, sin_s, ts)

    qd = zb_s[:, C_QD:C_KD].reshape(nb, dec_seq, DIFF_KV_HEADS, DIFF_REP, 2, HEAD_DIM)
    qd = qd.transpose(0, 2, 4, 3, 1, 5).reshape(nb, 4, Q_ROWS, HEAD_DIM)
    qs = zb_s[:, C_QS:C_KS].reshape(nb, dec_seq, SB_KV_HEADS, SB_REP, HEAD_DIM)
    qs = qs.transpose(0, 2, 3, 1, 4).reshape(nb, SB_KV_HEADS, Q_ROWS, HEAD_DIM)

    def new_page(a):
        return jnp.pad(a.reshape(nb, dec_seq * CHUNKS, LANES), ((0, 0), (0, (PAGE - dec_seq) * CHUNKS), (0, 0)))

    nvd = z32_s[:, C_VD:C_QS].reshape(nb, dec_seq, DIFF_KV_HEADS, 2, LANES).transpose(0, 1, 3, 2, 4)
    new_pages = (new_page(z32_s[:, C_KD:C_VD]), new_page(nvd),
                 new_page(z32_s[:, C_KS:C_VS]), new_page(z32_s[:, C_VS:C_GATE]))
    cvd = cache_v_diff[0].reshape(n_phys, PAGE, DIFF_KV_HEADS, 2, LANES).transpose(0, 1, 3, 2, 4)
    caches = (cache_k_diff[0].reshape(n_phys, PAGE * CHUNKS, LANES), cvd.reshape(n_phys, PAGE * CHUNKS, LANES),
              cache_k_sb[0].reshape(n_phys, PAGE * CHUNKS, LANES), cache_v_sb[0].reshape(n_phys, PAGE * CHUNKS, LANES))
    od_s, os_s = _decode(page_table, lams, subln_g, qd, qs, new_pages, caches)
    od_s = od_s.reshape(nb, DIFF_KV_HEADS, DIFF_REP, dec_seq, DIFF_V_DIM).transpose(0, 3, 1, 2, 4).reshape(ts, D_MODEL)
    os_s = os_s.reshape(nb, SB_KV_HEADS, SB_REP, dec_seq, HEAD_DIM).transpose(0, 3, 1, 2, 4).reshape(ts, D_MODEL)
    y_s = _post_attention(xs, od_s, os_s, z32_s, w, ts)

    def kv(z, rows, lead):
        return (z[:rows, C_KD:C_VD].reshape(lead + (DIFF_KV_HEADS, 2, HEAD_DIM)),
                z[:rows, C_VD:C_QS].reshape(lead + (DIFF_KV_HEADS, DIFF_V_DIM)),
                z[:rows, C_KS:C_VS].reshape(lead + (SB_KV_HEADS, HEAD_DIM)),
                z[:rows, C_VS:C_GATE].reshape(lead + (SB_KV_HEADS, HEAD_DIM)))

    y_prompt = y_p[N_META:tp_real].reshape(1, seq, D_MODEL)
    y_sample = y_s.reshape(nb, dec_seq, D_MODEL)
    return (y_prompt, y_sample) + kv(z32_p, tp_real, (1, 1, tp_real)) + kv(z32_s, ts, (1, nb, dec_seq))
```

```python
import functools
import math

import jax
import jax.numpy as jnp
import numpy as np
from jax import lax
from jax.experimental import pallas as pl
from jax.experimental.pallas import tpu as pltpu

F32 = jnp.float32
BF16 = jnp.bfloat16

D_MODEL = 2048
N_META = 16
HEAD_DIM = 128
DIFF_KV_HEADS = 2
DIFF_REP = 4
DIFF_V_DIM = 2 * HEAD_DIM
SB_KV_HEADS = 4
SB_REP = 4
ROPE_THETA = 10000.0
N_GROUPS = 4
EXPERTS_PER_GROUP = 8
N_EXPERTS = N_GROUPS * EXPERTS_PER_GROUP
D_EXPERT = 512
RMS_EPS = 1e-6
NEG_INF = -1e30
LAM_INIT = 0.8 - 0.6 * math.exp(-0.3 * 0)
SCALE = HEAD_DIM ** -0.5
LOG2E = math.log2(math.e)

C_QD, C_KD, C_VD, C_QS, C_KS, C_VS, C_GATE = 0, 2048, 2560, 3072, 5120, 5632, 6144
IN_COLS = 10240
ATTN_COLS = C_GATE

LANES = 128
PROJ_TN = 512
ROW_BLOCK = 768
ATT_BLOCK = 256
PAGES_PER_STEP = 16
VMEM_LIMIT = 56 * 1024 * 1024


def _cparams(sem):
    return pltpu.CompilerParams(dimension_semantics=sem, vmem_limit_bytes=VMEM_LIMIT)


def _dot(a, b):
    return jnp.dot(a, b, preferred_element_type=F32)


def _dot_nt(a, b):
    return lax.dot_general(a, b, (((1,), (1,)), ((), ())), preferred_element_type=F32)


def _rms(x, g):
    return x * lax.rsqrt(jnp.mean(x * x, axis=-1, keepdims=True) + RMS_EPS) * g


def _rope_table_kernel(pos_ref, inv_ref, cos_ref, sin_ref):
    ang = pos_ref[...] * inv_ref[...]
    lane = lax.broadcasted_iota(jnp.int32, ang.shape, 1)
    s = jnp.sin(ang)
    cos_ref[...] = jnp.cos(ang)
    sin_ref[...] = jnp.where(lane < HEAD_DIM // 2, -s, s)


def _rope_tables(pos, inv, tm):
    t = pos.shape[0]
    return pl.pallas_call(
        _rope_table_kernel,
        grid=(t // tm,),
        in_specs=[pl.BlockSpec((tm, 1), lambda i: (i, 0)), pl.BlockSpec((1, LANES), lambda i: (0, 0))],
        out_specs=[pl.BlockSpec((tm, LANES), lambda i: (i, 0))] * 2,
        out_shape=[jax.ShapeDtypeStruct((t, LANES), F32)] * 2,
        compiler_params=_cparams(("arbitrary",)),
        name="rope_tables",
    )(pos, inv)


N_ROPE_BLOCKS = C_VD // PROJ_TN
N_ATTN_BLOCKS = ATTN_COLS // PROJ_TN


def _proj_kernel(x_ref, g_ref, w_ref, cos_ref, sin_ref, z_ref, zb_ref, h_ref):
    j = pl.program_id(1)

    @pl.when(j == 0)
    def _():
        h_ref[...] = _rms(x_ref[...], g_ref[...]).astype(BF16)

    acc = _dot(h_ref[...], w_ref[...])

    @pl.when(j < N_ROPE_BLOCKS)
    def _():
        cos = cos_ref[...]
        sin = sin_ref[...]
        parts = []
        for c in range(PROJ_TN // HEAD_DIM):
            a = acc[:, c * HEAD_DIM:(c + 1) * HEAD_DIM]
            parts.append(a * cos + pltpu.roll(a, HEAD_DIM // 2, 1) * sin)
        r = jnp.concatenate(parts, axis=1)
        z_ref[...] = r
        zb_ref[...] = (r * jnp.where(j < C_KD // PROJ_TN, SCALE * LOG2E, 1.0)).astype(BF16)

    @pl.when((j >= N_ROPE_BLOCKS) & (j < N_ATTN_BLOCKS))
    def _():
        z_ref[...] = acc
        is_qs = (j >= C_QS // PROJ_TN) & (j < C_KS // PROJ_TN)
        zb_ref[...] = (acc * jnp.where(is_qs, SCALE, 1.0)).astype(BF16)

    @pl.when(j >= N_ATTN_BLOCKS)
    def _():
        z_ref[...] = acc


def _project(x, g, w_bf16, cos, sin, tm):
    t = x.shape[0]
    return pl.pallas_call(
        _proj_kernel,
        grid=(t // tm, IN_COLS // PROJ_TN),
        in_specs=[
            pl.BlockSpec((tm, D_MODEL), lambda i, j: (i, 0)),
            pl.BlockSpec((1, D_MODEL), lambda i, j: (0, 0)),
            pl.BlockSpec((D_MODEL, PROJ_TN), lambda i, j: (0, j)),
            pl.BlockSpec((tm, LANES), lambda i, j: (i, 0)),
            pl.BlockSpec((tm, LANES), lambda i, j: (i, 0)),
        ],
        out_specs=[
            pl.BlockSpec((tm, PROJ_TN), lambda i, j: (i, j)),
            pl.BlockSpec((tm, PROJ_TN), lambda i, j: (i, jnp.minimum(j, N_ATTN_BLOCKS - 1))),
        ],
        out_shape=[jax.ShapeDtypeStruct((t, IN_COLS), F32), jax.ShapeDtypeStruct((t, ATTN_COLS), BF16)],
        scratch_shapes=[pltpu.VMEM((tm, D_MODEL), BF16)],
        compiler_params=_cparams(("arbitrary", "arbitrary")),
        name="in_proj",
    )(x, g, w_bf16, cos, sin)


def _lambda(lq1, lk1, lq2, lk2):
    s1 = jnp.sum(lq1[...] * lk1[...], axis=-1, keepdims=True)
    s2 = jnp.sum(lq2[...] * lk2[...], axis=-1, keepdims=True)
    return jnp.exp(s1) - jnp.exp(s2) + LAM_INIT


def _diff_finish(acc1, l1, acc2, l2, lam, subln):
    w = acc1.shape[1]
    o = acc1 / _lanes(l1, w) - lam * (acc2 / _lanes(l2, w))
    o = o * lax.rsqrt(jnp.mean(o * o, axis=-1, keepdims=True) + RMS_EPS)
    return o * subln * (1.0 - LAM_INIT)


def _lanes(x, width):
    return x if width == LANES else jnp.concatenate([x] * (width // LANES), axis=1)


def _softmax_step(s, v, m_ref, l_ref, acc_ref, idx):
    m_old = m_ref[idx]
    m_new = jnp.maximum(m_old, jnp.max(s, axis=-1, keepdims=True))
    alpha = jnp.exp2(m_old - m_new)
    p = jnp.exp2(s - _lanes(m_new, s.shape[1]))
    l_ref[idx] = alpha * l_ref[idx] + jnp.sum(p, axis=-1, keepdims=True)
    acc_ref[idx] = _lanes(alpha, v.shape[1]) * acc_ref[idx] + _dot(p.astype(BF16), v)
    m_ref[idx] = m_new


def _strict_lower(n):
    r = lax.broadcasted_iota(jnp.int32, (n, n), 0)
    c = lax.broadcasted_iota(jnp.int32, (n, n), 1)
    return jnp.where(r > c, 1.0, 0.0).astype(BF16)


def _log_keep(z):
    return -(jnp.maximum(z, 0.0) + jnp.log(1.0 + jnp.exp(-jnp.abs(z))))


def _split_bf16(x):
    hi = x.astype(BF16)
    return hi, (x - hi.astype(F32)).astype(BF16)


SB_CUTOFF = -104.0


def _stick_step(z, v, u, keep, r_ref, acc_ref, idx):
    lk = _log_keep(z)
    lkm = lk if keep is None else jnp.where(keep, lk, 0.0)
    hi, lo = _split_bf16(lkm)
    suffix = _dot(hi, u) + _dot(lo, u)
    r = r_ref[idx]
    a = jnp.exp(z + lk + suffix + _lanes(r, z.shape[1]))
    if keep is not None:
        a = jnp.where(keep, a, 0.0)
    acc_ref[idx] = acc_ref[idx] + _dot(a.astype(BF16), v)
    r_ref[idx] = r + jnp.sum(lkm, axis=-1, keepdims=True)


def _diff_prompt_kernel(lq1, lk1, lq2, lk2, sg_ref, q_ref, k_ref, v_ref, o_ref, m_ref, l_ref, acc_ref):
    qi = pl.program_id(1)
    tb = ATT_BLOCK
    m_ref[...] = jnp.full(m_ref.shape, NEG_INF, F32)
    l_ref[...] = jnp.zeros(l_ref.shape, F32)
    acc_ref[...] = jnp.zeros(acc_ref.shape, F32)

    def block(kb, causal):
        start = pl.multiple_of(kb * tb, tb)
        kblk = k_ref[pl.ds(start, tb), :]
        vblk = v_ref[pl.ds(start, tb), :]
        for r in range(DIFF_REP):
            for m in range(2):
                c0 = (r * 2 + m) * HEAD_DIM
                s = _dot_nt(q_ref[:, c0:c0 + HEAD_DIM], kblk[:, m * HEAD_DIM:(m + 1) * HEAD_DIM])
                if causal is not None:
                    s = jnp.where(causal, s, NEG_INF)
                _softmax_step(s, vblk, m_ref, l_ref, acc_ref, r * 2 + m)

    def body(kb, carry):
        block(kb, None)
        return carry

    lax.fori_loop(0, qi, body, 0)
    row = lax.broadcasted_iota(jnp.int32, (tb, tb), 0)
    col = lax.broadcasted_iota(jnp.int32, (tb, tb), 1)
    block(qi, col <= row)

    lam = _lambda(lq1, lk1, lq2, lk2)
    for r in range(DIFF_REP):
        o = _diff_finish(acc_ref[2 * r], l_ref[2 * r], acc_ref[2 * r + 1], l_ref[2 * r + 1], lam, sg_ref[...])
        o_ref[:, r * DIFF_V_DIM:(r + 1) * DIFF_V_DIM] = o.astype(BF16)


def _lam_specs(nd):
    zero = (lambda *a: (0, 0))
    return [pl.BlockSpec((1, HEAD_DIM), zero)] * 4 + [pl.BlockSpec((1, DIFF_V_DIM), zero)]


def _diff_prompt(zb, lams, subln):
    t = zb.shape[0]
    tb = ATT_BLOCK
    qw = DIFF_REP * 2 * HEAD_DIM
    return pl.pallas_call(
        _diff_prompt_kernel,
        grid=(DIFF_KV_HEADS, t // tb),
        in_specs=_lam_specs(2) + [
            pl.BlockSpec((tb, qw), lambda g, i: (i, g)),
            pl.BlockSpec((t, 2 * HEAD_DIM), lambda g, i: (0, C_KD // (2 * HEAD_DIM) + g)),
            pl.BlockSpec((t, DIFF_V_DIM), lambda g, i: (0, C_VD // DIFF_V_DIM + g)),
        ],
        out_specs=pl.BlockSpec((tb, DIFF_REP * DIFF_V_DIM), lambda g, i: (i, g)),
        out_shape=jax.ShapeDtypeStruct((t, D_MODEL), BF16),
        scratch_shapes=[
            pltpu.VMEM((2 * DIFF_REP, tb, LANES), F32),
            pltpu.VMEM((2 * DIFF_REP, tb, LANES), F32),
            pltpu.VMEM((2 * DIFF_REP, tb, DIFF_V_DIM), F32),
        ],
        compiler_params=_cparams(("arbitrary", "arbitrary")),
        name="diff_prompt",
    )(*lams, subln, zb, zb, zb)


def _sb_prompt_kernel(q_ref, k_ref, v_ref, o_ref, r_ref, acc_ref):
    qi = pl.program_id(1)
    tb = ATT_BLOCK
    r_ref[...] = jnp.zeros(r_ref.shape, F32)
    acc_ref[...] = jnp.zeros(acc_ref.shape, F32)
    u = _strict_lower(tb)

    def block(kb, keep):
        start = pl.multiple_of(kb * tb, tb)
        kblk = k_ref[pl.ds(start, tb), :]
        vblk = v_ref[pl.ds(start, tb), :]
        for r in range(SB_REP):
            z = _dot_nt(q_ref[:, r * HEAD_DIM:(r + 1) * HEAD_DIM], kblk)
            _stick_step(z, vblk, u, keep, r_ref, acc_ref, r)

    row = lax.broadcasted_iota(jnp.int32, (tb, tb), 0)
    col = lax.broadcasted_iota(jnp.int32, (tb, tb), 1)
    block(qi, col < row)

    def live(c):
        return (c[0] >= 0) & (c[1] > SB_CUTOFF)

    def body(c):
        block(c[0], None)
        return c[0] - 1, jnp.max(r_ref[...])

    lax.while_loop(live, body, (qi - 1, jnp.max(r_ref[...])))
    for r in range(SB_REP):
        o_ref[:, r * HEAD_DIM:(r + 1) * HEAD_DIM] = acc_ref[r].astype(BF16)


def _sb_prompt(zb):
    t = zb.shape[0]
    tb = ATT_BLOCK
    qw = SB_REP * HEAD_DIM
    return pl.pallas_call(
        _sb_prompt_kernel,
        grid=(SB_KV_HEADS, t // tb),
        in_specs=[
            pl.BlockSpec((tb, qw), lambda h, i: (i, C_QS // qw + h)),
            pl.BlockSpec((t, HEAD_DIM), lambda h, i: (0, C_KS // HEAD_DIM + h)),
            pl.BlockSpec((t, HEAD_DIM), lambda h, i: (0, C_VS // HEAD_DIM + h)),
        ],
        out_specs=pl.BlockSpec((tb, qw), lambda h, i: (i, h)),
        out_shape=jax.ShapeDtypeStruct((t, D_MODEL), BF16),
        scratch_shapes=[pltpu.VMEM((SB_REP, tb, LANES), F32), pltpu.VMEM((SB_REP, tb, HEAD_DIM), F32)],
        compiler_params=_cparams(("arbitrary", "arbitrary")),
        name="sb_prompt",
    )(zb, zb, zb)


PAGE = 128
CHUNKS = 4
Q_ROWS = 16
DEC_SEQ = 4


def _page_chunk(ref, c):
    return ref[pl.ds(c, PAGE, stride=CHUNKS), :].astype(BF16)


def _decode_diff(qd_ref, kds, vds, causal, m_ref, l_ref, acc_ref):
    n = len(kds)
    for g in range(DIFF_KV_HEADS):
        v = [[_page_chunk(vd, c * 2 + g) for c in range(2)] for vd in vds]
        alphas, pbs = [], []
        for m in range(2):
            idx = g * 2 + m
            q = qd_ref[idx]
            s = [_dot_nt(q, _page_chunk(kd, idx)) for kd in kds]
            if causal is not None:
                s = [jnp.where(causal, x, NEG_INF) for x in s]
            m_old = m_ref[idx]
            m_new = jnp.maximum(m_old, jnp.max(functools.reduce(jnp.maximum, s), axis=-1, keepdims=True))
            alpha = jnp.exp2(m_old - m_new)
            p = [jnp.exp2(x - m_new) for x in s]
            l_ref[idx] = alpha * l_ref[idx] + jnp.sum(functools.reduce(jnp.add, p), axis=-1, keepdims=True)
            m_ref[idx] = m_new
            alphas.append(alpha)
            pbs.append([x.astype(BF16) for x in p])
        pb = [jnp.concatenate([pbs[0][i], pbs[1][i]], axis=0) for i in range(n)]
        pv = [functools.reduce(jnp.add, [_dot(pb[i], v[i][c]) for i in range(n)]) for c in range(2)]
        pv = jnp.concatenate(pv, axis=1)
        for m in range(2):
            idx = g * 2 + m
            acc_ref[idx] = _lanes(alphas[m], DIFF_V_DIM) * acc_ref[idx] + pv[m * Q_ROWS:(m + 1) * Q_ROWS]


def _decode_stick(qs_ref, kss, vss, strict, u, r_ref, acc_ref):
    n = len(kss)
    rows = n * Q_ROWS
    for h in range(SB_KV_HEADS):
        q = qs_ref[h]
        z = jnp.concatenate([_dot_nt(q, _page_chunk(ks, h)) for ks in kss], axis=0)
        lk = _log_keep(z)
        keep = None if strict is None else jnp.concatenate([strict] * n, axis=0)
        lkm = lk if keep is None else jnp.where(keep, lk, 0.0)
        hi, lo = _split_bf16(lkm)
        s2 = _dot(jnp.concatenate([hi, lo], axis=0), u)
        suffix = s2[:rows] + s2[rows:]
        tot = jnp.sum(lkm, axis=-1, keepdims=True)
        r = r_ref[h]
        offs = []
        for i in range(n):
            offs.append(r)
            r = r + tot[i * Q_ROWS:(i + 1) * Q_ROWS]
        a = jnp.exp(z + lk + suffix + jnp.concatenate(offs, axis=0))
        if keep is not None:
            a = jnp.where(keep, a, 0.0)
        ab = a.astype(BF16)
        pv = [_dot(ab[i * Q_ROWS:(i + 1) * Q_ROWS], _page_chunk(vss[i], h)) for i in range(n)]
        acc_ref[h] = acc_ref[h] + functools.reduce(jnp.add, pv)
        r_ref[h] = r


def _decode_kernel(pt_ref, lq1, lk1, lq2, lk2, sg_ref, qd_ref, qs_ref, nkd, nvd, nks, nvs, *rest):
    n_in = 2 * PAGES_PER_STEP
    pages = rest[:n_in]
    ks_hbm, vs_hbm, od_ref, os_ref = rest[n_in:n_in + 4]
    m_ref, l_ref, accd_ref, r_ref, accs_ref, ks_buf, vs_buf, sb_sem = rest[n_in + 4:]
    b_id = pl.program_id(0)
    s_id = pl.program_id(1)
    u = _strict_lower(PAGE)

    @pl.when(s_id == 0)
    def _():
        m_ref[...] = jnp.full(m_ref.shape, NEG_INF, F32)
        l_ref[...] = jnp.zeros(l_ref.shape, F32)
        accd_ref[...] = jnp.zeros(accd_ref.shape, F32)
        r_ref[...] = jnp.zeros(r_ref.shape, F32)
        accs_ref[...] = jnp.zeros(accs_ref.shape, F32)
        tq = lax.broadcasted_iota(jnp.int32, (Q_ROWS, PAGE), 0) & (DEC_SEQ - 1)
        key = lax.broadcasted_iota(jnp.int32, (Q_ROWS, PAGE), 1)
        _decode_diff(qd_ref, [nkd], [nvd], key <= tq, m_ref, l_ref, accd_ref)
        _decode_stick(qs_ref, [nks], [nvs], key < tq, u, r_ref, accs_ref)

    _decode_diff(qd_ref, pages[0::2], pages[1::2], None, m_ref, l_ref, accd_ref)

    @pl.when(jnp.max(r_ref[...]) > SB_CUTOFF)
    def _():
        n_pages = pl.num_programs(1) * PAGES_PER_STEP

        def copies(p):
            page = pt_ref[b_id, n_pages - 1 - (s_id * PAGES_PER_STEP + p)]
            return (pltpu.make_async_copy(ks_hbm.at[page], ks_buf.at[p], sb_sem),
                    pltpu.make_async_copy(vs_hbm.at[page], vs_buf.at[p], sb_sem))

        for p in range(PAGES_PER_STEP):
            for c in copies(p):
                c.start()
        for p in range(PAGES_PER_STEP):
            for c in copies(p):
                c.wait()
        bufs = range(PAGES_PER_STEP)
        _decode_stick(qs_ref, [ks_buf.at[p] for p in bufs], [vs_buf.at[p] for p in bufs], None, u, r_ref, accs_ref)

    @pl.when(s_id == pl.num_programs(1) - 1)
    def _():
        lam = _lambda(lq1, lk1, lq2, lk2)
        for g in range(DIFF_KV_HEADS):
            o = _diff_finish(accd_ref[2 * g], l_ref[2 * g], accd_ref[2 * g + 1], l_ref[2 * g + 1], lam, sg_ref[...])
            od_ref[g] = o.astype(BF16)
        for h in range(SB_KV_HEADS):
            os_ref[h] = accs_ref[h].astype(BF16)


def _decode(page_table, lams, subln, qd, qs, new_pages, caches):
    nb, n_pages = page_table.shape
    steps = n_pages // PAGES_PER_STEP
    zero2 = lambda b, s, pt: (0, 0)
    per_b4 = lambda b, s, pt: (b, 0, 0, 0)
    per_b3 = lambda b, s, pt: (b, 0, 0)

    def page_spec(p):
        return pl.BlockSpec((None, PAGE * CHUNKS, LANES),
                            lambda b, s, pt: (pt[b, n_pages - 1 - (s * PAGES_PER_STEP + p)], 0, 0))

    in_specs = ([pl.BlockSpec((1, HEAD_DIM), zero2)] * 4 + [pl.BlockSpec((1, DIFF_V_DIM), zero2)]
                + [pl.BlockSpec((None, 4, Q_ROWS, HEAD_DIM), per_b4)] * 2
                + [pl.BlockSpec((None, PAGE * CHUNKS, LANES), per_b3)] * 4)
    page_args = []
    for p in range(PAGES_PER_STEP):
        in_specs += [page_spec(p)] * 2
        page_args += list(caches[:2])
    in_specs += [pl.BlockSpec(memory_space=pl.ANY)] * 2
    page_args += list(caches[2:])
    grid_spec = pltpu.PrefetchScalarGridSpec(
        num_scalar_prefetch=1,
        grid=(nb, steps),
        in_specs=in_specs,
        out_specs=[pl.BlockSpec((None, DIFF_KV_HEADS, Q_ROWS, DIFF_V_DIM), per_b4),
                   pl.BlockSpec((None, SB_KV_HEADS, Q_ROWS, HEAD_DIM), per_b4)],
        scratch_shapes=[
            pltpu.VMEM((4, Q_ROWS, LANES), F32), pltpu.VMEM((4, Q_ROWS, LANES), F32),
            pltpu.VMEM((4, Q_ROWS, DIFF_V_DIM), F32),
            pltpu.VMEM((SB_KV_HEADS, Q_ROWS, LANES), F32), pltpu.VMEM((SB_KV_HEADS, Q_ROWS, HEAD_DIM), F32),
            pltpu.VMEM((PAGES_PER_STEP, PAGE * CHUNKS, LANES), F32),
            pltpu.VMEM((PAGES_PER_STEP, PAGE * CHUNKS, LANES), F32),
            pltpu.SemaphoreType.DMA(()),
        ],
    )
    return pl.pallas_call(
        _decode_kernel,
        grid_spec=grid_spec,
        out_shape=[jax.ShapeDtypeStruct((nb, DIFF_KV_HEADS, Q_ROWS, DIFF_V_DIM), BF16),
                   jax.ShapeDtypeStruct((nb, SB_KV_HEADS, Q_ROWS, HEAD_DIM), BF16)],
        compiler_params=_cparams(("arbitrary", "arbitrary")),
        name="decode_attn",
    )(page_table, *lams, subln, qd, qs, *new_pages, *page_args)


def _merge_kernel(od_ref, os_ref, gd_ref, gs_ref, wd_ref, ws_ref, o_ref):
    g_d = 1.0 / (1.0 + jnp.exp(-gd_ref[...]))
    g_s = 1.0 / (1.0 + jnp.exp(-gs_ref[...]))
    o_ref[...] = (g_d * _dot(od_ref[...], wd_ref[...]) + g_s * _dot(os_ref[...], ws_ref[...])).astype(BF16)


def _merge(od, os_, z32, wbd, wbs, tm):
    t = od.shape[0]
    tn = PROJ_TN
    gd0 = C_GATE // tn
    gs0 = (C_GATE + D_MODEL) // tn
    return pl.pallas_call(
        _merge_kernel,
        grid=(t // tm, D_MODEL // tn),
        in_specs=[
            pl.BlockSpec((tm, D_MODEL), lambda i, j: (i, 0)),
            pl.BlockSpec((tm, D_MODEL), lambda i, j: (i, 0)),
            pl.BlockSpec((tm, tn), lambda i, j: (i, gd0 + j)),
            pl.BlockSpec((tm, tn), lambda i, j: (i, gs0 + j)),
            pl.BlockSpec((D_MODEL, tn), lambda i, j: (0, j)),
            pl.BlockSpec((D_MODEL, tn), lambda i, j: (0, j)),
        ],
        out_specs=pl.BlockSpec((tm, tn), lambda i, j: (i, j)),
        out_shape=jax.ShapeDtypeStruct((t, D_MODEL), BF16),
        compiler_params=_cparams(("arbitrary", "arbitrary")),
        name="merge",
    )(od, os_, z32, z32, wbd, wbs)


def _out_kernel(x_ref, m_ref, w_ref, o_ref):
    o_ref[...] = x_ref[...] + _dot(m_ref[...], w_ref[...])


def _out_proj(x, merged, w_out, tm):
    t = x.shape[0]
    tn = PROJ_TN
    return pl.pallas_call(
        _out_kernel,
        grid=(t // tm, D_MODEL // tn),
        in_specs=[
            pl.BlockSpec((tm, tn), lambda i, j: (i, j)),
            pl.BlockSpec((tm, D_MODEL), lambda i, j: (i, 0)),
            pl.BlockSpec((D_MODEL, tn), lambda i, j: (0, j)),
        ],
        out_specs=pl.BlockSpec((tm, tn), lambda i, j: (i, j)),
        out_shape=jax.ShapeDtypeStruct((t, D_MODEL), F32),
        compiler_params=_cparams(("arbitrary", "arbitrary")),
        name="out_proj",
    )(x, merged, w_out)


ROW_CHUNKS = D_MODEL // LANES


def _store_row_major(ref, x):
    rows = x.shape[0]
    for c in range(ROW_CHUNKS):
        ref[pl.ds(c, rows, stride=ROW_CHUNKS), :] = x[:, c * LANES:(c + 1) * LANES]


def _load_row_major(ref, rows):
    return jnp.concatenate([ref[pl.ds(c, rows, stride=ROW_CHUNKS), :] for c in range(ROW_CHUNKS)], axis=1)


def _router_kernel(x_ref, g_ref, w_ref, b_ref, hn_ref, comb_ref, route_ref, counts_ref, carry_ref):
    hn32 = _rms(x_ref[...], g_ref[...])
    hn = hn32.astype(BF16)
    if hn_ref.dtype == BF16:
        hn_ref[...] = hn
    else:
        _store_row_major(hn_ref, hn32)
    logits = _dot(hn, w_ref[...]) + b_ref[...]
    gl = logits[:, :LANES]
    el = logits[:, LANES:]
    lane = lax.broadcasted_iota(jnp.int32, gl.shape, 1).astype(F32)
    far = float(LANES)

    def first_max(v):
        mx = jnp.max(v, axis=-1, keepdims=True)
        return mx, jnp.min(jnp.where(v == mx, lane, far), axis=-1, keepdims=True)

    gl = jnp.where(lane < N_GROUPS, gl, NEG_INF)
    gmax, grp = first_max(gl)
    p_grp = 1.0 / jnp.sum(jnp.exp(gl - gmax), axis=-1, keepdims=True)
    lo = grp * EXPERTS_PER_GROUP
    es = jnp.where((lane >= lo) & (lane < lo + EXPERTS_PER_GROUP), el, NEG_INF)
    v1, i1 = first_max(es)
    es2 = jnp.where(lane == i1, NEG_INF, es)
    v2, i2 = first_max(es2)
    e2 = jnp.exp(v2 - v1)
    w1 = 1.0 / (1.0 + e2)
    w2 = e2 / (1.0 + e2)
    ga = p_grp * w1
    gb = p_grp * w2
    oh_a = jnp.where(lane == i1, 1.0, 0.0)
    oh_b = jnp.where(lane == i2, 1.0, 0.0)
    comb_ref[...] = ga * oh_a + gb * oh_b

    @pl.when(pl.program_id(0) == 0)
    def _():
        carry_ref[...] = jnp.zeros(carry_ref.shape, F32)

    tm = x_ref.shape[0]
    oh = oh_a + oh_b
    before = _dot(_strict_lower(tm), oh.astype(BF16)) + carry_ref[...]
    rank_a = jnp.sum(before * oh_a, axis=-1, keepdims=True)
    rank_b = jnp.sum(before * oh_b, axis=-1, keepdims=True)
    carry_ref[...] = carry_ref[...] + jnp.sum(oh, axis=0, keepdims=True)
    counts_ref[...] = carry_ref[...]
    fields = (i1, i2, ga, gb, rank_a, rank_b)
    route_ref[...] = functools.reduce(jnp.add, [jnp.where(lane == float(k), f, 0.0) for k, f in enumerate(fields)])


def _router(x2, g, w_r, b_r, tm, routed):
    t = x2.shape[0]
    if routed:
        hn_spec = pl.BlockSpec((tm * ROW_CHUNKS, LANES), lambda i: (i, 0))
        hn_shape = jax.ShapeDtypeStruct((t * ROW_CHUNKS, LANES), F32)
    else:
        hn_spec = pl.BlockSpec((tm, D_MODEL), lambda i: (i, 0))
        hn_shape = jax.ShapeDtypeStruct((t, D_MODEL), BF16)
    return pl.pallas_call(
        _router_kernel,
        grid=(t // tm,),
        in_specs=[
            pl.BlockSpec((tm, D_MODEL), lambda i: (i, 0)),
            pl.BlockSpec((1, D_MODEL), lambda i: (0, 0)),
            pl.BlockSpec((D_MODEL, 2 * LANES), lambda i: (0, 0)),
            pl.BlockSpec((1, 2 * LANES), lambda i: (0, 0)),
        ],
        out_specs=[hn_spec, pl.BlockSpec((tm, LANES), lambda i: (i, 0)),
                   pl.BlockSpec((tm, LANES), lambda i: (i, 0)), pl.BlockSpec((1, LANES), lambda i: (0, 0))],
        out_shape=[hn_shape, jax.ShapeDtypeStruct((t, LANES), F32),
                   jax.ShapeDtypeStruct((t, LANES), F32), jax.ShapeDtypeStruct((1, LANES), F32)],
        scratch_shapes=[pltpu.VMEM((1, LANES), F32)],
        compiler_params=_cparams(("arbitrary",)),
        name="router",
    )(x2, g, w_r, b_r)


MOE_TM = 256


DMA_UNROLL = 8


def _moe_routed_kernel(te_ref, dprev_ref, dcur_ref, dnext_ref, wg_ref, wu_ref, wd_ref, hn_hbm, y_hbm,
                       xbuf, ybuf, gsem, ssem, *, n):
    j = pl.program_id(0)
    last = pl.num_programs(0) - 1
    cur = j % 2
    next_used = (j < last) & (te_ref[jnp.minimum(j + 1, last)] < N_EXPERTS)

    def rows_at(r):
        return pl.ds(pl.multiple_of(r * ROW_CHUNKS, ROW_CHUNKS), ROW_CHUNKS)

    def row_in(dref, b, r):
        d = dref[0, r]
        src = jnp.where(d < 0, 0, jnp.where(d >= n, d - n, d))
        return pltpu.make_async_copy(hn_hbm.at[rows_at(src), :], xbuf.at[b, rows_at(r), :], gsem.at[b])

    def row_out(dref, b, r):
        return pltpu.make_async_copy(ybuf.at[b, rows_at(r), :], y_hbm.at[rows_at(dref[0, r]), :], ssem.at[b])

    def each_row(fn):
        def body(r, c):
            fn(r)
            return c
        lax.fori_loop(0, MOE_TM, body, 0, unroll=DMA_UNROLL)

    def each_real_row(dref, fn):
        each_row(lambda r: pl.when(dref[0, r] >= 0)(lambda: fn(r)))

    @pl.when(te_ref[j] < N_EXPERTS)
    def _():
        @pl.when(j == 0)
        def _():
            each_row(lambda r: row_in(dcur_ref, cur, r).start())

        @pl.when(next_used)
        def _():
            each_row(lambda r: row_in(dnext_ref, 1 - cur, r).start())

        each_row(lambda r: row_in(dcur_ref, cur, r).wait())
        x = _load_row_major(xbuf.at[cur], MOE_TM).astype(BF16)
        gate = _dot(x, wg_ref[...].astype(BF16))
        hid = gate * (1.0 / (1.0 + jnp.exp(-gate))) * _dot(x, wu_ref[...].astype(BF16))
        _store_row_major(ybuf.at[cur], _dot(hid.astype(BF16), wd_ref[...].astype(BF16)))

        @pl.when(j > 0)
        def _():
            each_real_row(dprev_ref, lambda r: row_out(dprev_ref, 1 - cur, r).wait())

        each_real_row(dcur_ref, lambda r: row_out(dcur_ref, cur, r).start())

        @pl.when(jnp.logical_not(next_used))
        def _():
            each_real_row(dcur_ref, lambda r: row_out(dcur_ref, cur, r).wait())


def _moe_routed(hn, route, counts, wg, wu, wd):
    n = hn.shape[0] // ROW_CHUNKS
    n_tiles = (2 * n) // MOE_TM + N_EXPERTS
    p = n_tiles * MOE_TM
    experts = route[:, 0:2].astype(jnp.int32)
    ranks = route[:, 4:6].astype(jnp.int32)
    cnt = counts[0, :N_EXPERTS].astype(jnp.int32)
    padded = (cnt + MOE_TM - 1) // MOE_TM * MOE_TM
    ends = jnp.cumsum(padded)
    slot = ((ends - padded)[experts] + ranks).reshape(-1)
    tok = jnp.arange(n, dtype=jnp.int32)
    dst = jnp.full((p,), -1, jnp.int32).at[slot].set(jnp.stack([tok, tok + n], axis=1).reshape(-1))
    tile_expert = jnp.sum(jnp.arange(n_tiles, dtype=jnp.int32)[:, None] * MOE_TM >= ends[None, :], axis=1)
    tile_expert = tile_expert.astype(jnp.int32)

    def w_spec(shape):
        return pl.BlockSpec((None,) + shape, lambda j, te: (jnp.minimum(te[j], N_EXPERTS - 1), 0, 0))

    def dst_spec(shift):
        return pl.BlockSpec((None, 1, MOE_TM), lambda j, te: (jnp.clip(j + shift, 0, n_tiles - 1), 0, 0),
                            memory_space=pltpu.SMEM)

    grid_spec = pltpu.PrefetchScalarGridSpec(
        num_scalar_prefetch=1,
        grid=(n_tiles,),
        in_specs=[dst_spec(-1), dst_spec(0), dst_spec(1),
                  w_spec((D_MODEL, D_EXPERT)), w_spec((D_MODEL, D_EXPERT)), w_spec((D_EXPERT, D_MODEL)),
                  pl.BlockSpec(memory_space=pl.ANY)],
        out_specs=pl.BlockSpec(memory_space=pl.ANY),
        scratch_shapes=[pltpu.VMEM((2, MOE_TM * ROW_CHUNKS, LANES), F32),
                        pltpu.VMEM((2, MOE_TM * ROW_CHUNKS, LANES), F32),
                        pltpu.SemaphoreType.DMA((2,)), pltpu.SemaphoreType.DMA((2,))],
    )
    dst = dst.reshape(n_tiles, 1, MOE_TM)
    return pl.pallas_call(
        functools.partial(_moe_routed_kernel, n=n),
        grid_spec=grid_spec,
        out_shape=jax.ShapeDtypeStruct((2 * n * ROW_CHUNKS, LANES), F32),
        compiler_params=_cparams(("arbitrary",)),
        name="moe_routed",
    )(tile_expert, dst, dst, dst, wg, wu, wd, hn)


def _combine_kernel(x_ref, route_ref, ya_ref, yb_ref, g_ref, o_ref):
    tm = x_ref.shape[0]
    route = route_ref[...]
    lane = lax.broadcasted_iota(jnp.int32, route.shape, 1)
    ga = jnp.sum(jnp.where(lane == 2, route, 0.0), axis=-1, keepdims=True)
    gb = jnp.sum(jnp.where(lane == 3, route, 0.0), axis=-1, keepdims=True)
    moe = ga * _load_row_major(ya_ref, tm) + gb * _load_row_major(yb_ref, tm)
    o_ref[...] = _rms(x_ref[...] + moe, g_ref[...])


def _combine(x2, route, y2, gf, tm):
    t = x2.shape[0]
    nblk = t // tm
    return pl.pallas_call(
        _combine_kernel,
        grid=(nblk,),
        in_specs=[pl.BlockSpec((tm, D_MODEL), lambda i: (i, 0)),
                  pl.BlockSpec((tm, LANES), lambda i: (i, 0)),
                  pl.BlockSpec((tm * ROW_CHUNKS, LANES), lambda i: (i, 0)),
                  pl.BlockSpec((tm * ROW_CHUNKS, LANES), lambda i: (i + nblk, 0)),
                  pl.BlockSpec((1, D_MODEL), lambda i: (0, 0))],
        out_specs=pl.BlockSpec((tm, D_MODEL), lambda i: (i, 0)),
        out_shape=jax.ShapeDtypeStruct((t, D_MODEL), F32),
        compiler_params=_cparams(("arbitrary",)),
        name="moe_combine",
    )(x2, route, y2, y2, gf)


def _moe_kernel(hn_ref, comb_ref, x_ref, wg_ref, wu_ref, wd_ref, gf_ref, y_ref):
    e = pl.program_id(1)

    @pl.when(e == 0)
    def _():
        y_ref[...] = x_ref[...]

    hn = hn_ref[...]
    lane = lax.broadcasted_iota(jnp.int32, comb_ref.shape, 1)
    c = jnp.sum(jnp.where(lane == e, comb_ref[...], 0.0), axis=-1, keepdims=True)
    gate = _dot(hn, wg_ref[...].astype(BF16))
    hid = gate * (1.0 / (1.0 + jnp.exp(-gate))) * _dot(hn, wu_ref[...].astype(BF16))
    y_ref[...] += _dot((hid * c).astype(BF16), wd_ref[...].astype(BF16))

    @pl.when(e == pl.num_programs(1) - 1)
    def _():
        y_ref[...] = _rms(y_ref[...], gf_ref[...])


def _moe(hn, comb, x2, wg, wu, wd, gf, tm):
    t = hn.shape[0]
    return pl.pallas_call(
        _moe_kernel,
        grid=(t // tm, N_EXPERTS),
        in_specs=[
            pl.BlockSpec((tm, D_MODEL), lambda i, e: (i, 0)),
            pl.BlockSpec((tm, LANES), lambda i, e: (i, 0)),
            pl.BlockSpec((tm, D_MODEL), lambda i, e: (i, 0)),
            pl.BlockSpec((None, D_MODEL, D_EXPERT), lambda i, e: (e, 0, 0)),
            pl.BlockSpec((None, D_MODEL, D_EXPERT), lambda i, e: (e, 0, 0)),
            pl.BlockSpec((None, D_EXPERT, D_MODEL), lambda i, e: (e, 0, 0)),
            pl.BlockSpec((1, D_MODEL), lambda i, e: (0, 0)),
        ],
        out_specs=pl.BlockSpec((tm, D_MODEL), lambda i, e: (i, 0)),
        out_shape=jax.ShapeDtypeStruct((t, D_MODEL), F32),
        compiler_params=_cparams(("arbitrary", "arbitrary")),
        name="moe",
    )(hn, comb, x2, wg, wu, wd, gf)


def _post_attention(x, od, os_, z32, w, tm):
    merged = _merge(od, os_, z32, w["wbd"], w["wbs"], tm)
    x2 = _out_proj(x, merged, w["wout"], tm)
    if x.shape[0] // MOE_TM < N_EXPERTS:
        hn, comb, _, _ = _router(x2, w["g_ffn"], w["w_r"], w["b_r"], tm, False)
        return _moe(hn, comb, x2, w["wg"], w["wu"], w["wd"], w["g_final"], tm)
    hn, _, route, counts = _router(x2, w["g_ffn"], w["w_r"], w["b_r"], tm, True)
    y2 = _moe_routed(hn, route, counts, w["wg"], w["wu"], w["wd"])
    return _combine(x2, route, y2, w["g_final"], tm)


def _pad_rows(a, rows):
    return jnp.pad(a, ((0, rows - a.shape[0]),) + ((0, 0),) * (a.ndim - 1))


def kernel(x_prompt, x_sample, cache_k_diff, cache_v_diff, cache_k_sb, cache_v_sb, page_table, meta_tokens, norm_mix_g, w_in, lambda_q1, lambda_k1, lambda_q2, lambda_k2, subln_g, w_branch_diff, w_branch_sb, w_out, norm_ffn_g, w_router_group, b_router_group, w_router_expert, b_router_expert, w_expert_gate, w_expert_up, w_expert_down, norm_final_g):
    assert x_prompt.shape[0] == 1 and norm_mix_g.shape[0] == 1
    seq = x_prompt.shape[1]
    tp_real = N_META + seq
    tp = -(-tp_real // ROW_BLOCK) * ROW_BLOCK
    assert tp % ATT_BLOCK == 0
    nb, dec_seq, _ = x_sample.shape
    ts = nb * dec_seq
    n_phys = cache_k_diff.shape[1]
    past_len = page_table.shape[1] * PAGE
    assert dec_seq == DEC_SEQ and cache_k_diff.shape[2] == PAGE

    w_in_b = w_in[0].astype(BF16)
    w_r = jnp.zeros((D_MODEL, 2 * LANES), F32)
    w_r = w_r.at[:, :N_GROUPS].set(w_router_group[0]).at[:, LANES:LANES + N_EXPERTS].set(w_router_expert[0])
    b_r = jnp.zeros((1, 2 * LANES), F32)
    b_r = b_r.at[0, :N_GROUPS].set(b_router_group[0]).at[0, LANES:LANES + N_EXPERTS].set(b_router_expert[0])
    w = dict(
        wbd=w_branch_diff[0].astype(BF16), wbs=w_branch_sb[0].astype(BF16), wout=w_out[0].astype(BF16),
        g_ffn=norm_ffn_g, w_r=w_r.astype(BF16), b_r=b_r,
        wg=w_expert_gate[0], wu=w_expert_up[0], wd=w_expert_down[0],
        g_final=norm_final_g.reshape(1, D_MODEL),
    )
    lams = (lambda_q1, lambda_k1, lambda_q2, lambda_k2)

    half = HEAD_DIM // 2
    inv = ROPE_THETA ** (-jnp.arange(half, dtype=F32) * 2.0 / HEAD_DIM)
    inv = jnp.concatenate([inv, inv]).reshape(1, LANES)

    xp = _pad_rows(jnp.concatenate([meta_tokens.astype(F32), x_prompt[0]], axis=0), tp)
    pos_p = jnp.arange(tp, dtype=jnp.int32).astype(F32).reshape(tp, 1)
    cos_p, sin_p = _rope_tables(pos_p, inv, ROW_BLOCK)
    z32_p, zb_p = _project(xp, norm_mix_g, w_in_b, cos_p, sin_p, ROW_BLOCK)
    od_p = _diff_prompt(zb_p, lams, subln_g)
    os_p = _sb_prompt(zb_p)
    y_p = _post_attention(xp, od_p, os_p, z32_p, w, ROW_BLOCK)

    xs = x_sample.reshape(ts, D_MODEL)
    pos_s = jnp.tile(past_len + jnp.arange(dec_seq, dtype=jnp.int32), nb).astype(F32).reshape(ts, 1)
    cos_s, sin_s = _rope_tables(pos_s, inv, ts)
    z32_s, zb_s = _project(xs, norm_mix_g, w_in_b, cos_s, sin_s, ts)

    qd = zb_s[:, C_QD:C_KD].reshape(nb, dec_seq, DIFF_KV_HEADS, DIFF_REP, 2, HEAD_DIM)
    qd = qd.transpose(0, 2, 4, 3, 1, 5).reshape(nb, 4, Q_ROWS, HEAD_DIM)
    qs = zb_s[:, C_QS:C_KS].reshape(nb, dec_seq, SB_KV_HEADS, SB_REP, HEAD_DIM)
    qs = qs.transpose(0, 2, 3, 1, 4).reshape(nb, SB_KV_HEADS, Q_ROWS, HEAD_DIM)

    def new_page(a):
        return jnp.pad(a.reshape(nb, dec_seq * CHUNKS, LANES), ((0, 0), (0, (PAGE - dec_seq) * CHUNKS), (0, 0)))

    nvd = z32_s[:, C_VD:C_QS].reshape(nb, dec_seq, DIFF_KV_HEADS, 2, LANES).transpose(0, 1, 3, 2, 4)
    new_pages = (new_page(z32_s[:, C_KD:C_VD]), new_page(nvd),
                 new_page(z32_s[:, C_KS:C_VS]), new_page(z32_s[:, C_VS:C_GATE]))
    cvd = cache_v_diff[0].reshape(n_phys, PAGE, DIFF_KV_HEADS, 2, LANES).transpose(0, 1, 3, 2, 4)
    caches = (cache_k_diff[0].reshape(n_phys, PAGE * CHUNKS, LANES), cvd.reshape(n_phys, PAGE * CHUNKS, LANES),
              cache_k_sb[0].reshape(n_phys, PAGE * CHUNKS, LANES), cache_v_sb[0].reshape(n_phys, PAGE * CHUNKS, LANES))
    od_s, os_s = _decode(page_table, lams, subln_g, qd, qs, new_pages, caches)
    od_s = od_s.reshape(nb, DIFF_KV_HEADS, DIFF_REP, dec_seq, DIFF_V_DIM).transpose(0, 3, 1, 2, 4).reshape(ts, D_MODEL)
    os_s = os_s.reshape(nb, SB_KV_HEADS, SB_REP, dec_seq, HEAD_DIM).transpose(0, 3, 1, 2, 4).reshape(ts, D_MODEL)
    y_s = _post_attention(xs, od_s, os_s, z32_s, w, ts)

    def kv(z, rows, lead):
        return (z[:rows, C_KD:C_VD].reshape(lead + (DIFF_KV_HEADS, 2, HEAD_DIM)),
                z[:rows, C_VD:C_QS].reshape(lead + (DIFF_KV_HEADS, DIFF_V_DIM)),
                z[:rows, C_KS:C_VS].reshape(lead + (SB_KV_HEADS, HEAD_DIM)),
                z[:rows, C_VS:C_GATE].reshape(lead + (SB_KV_HEADS, HEAD_DIM)))

    y_prompt = y_p[N_META:tp_real].reshape(1, seq, D_MODEL)
    y_sample = y_s.reshape(nb, dec_seq, D_MODEL)
    return (y_prompt, y_sample) + kv(z32_p, tp_real, (1, 1, tp_real)) + kv(z32_s, ts, (1, nb, dec_seq))
```

```python
import functools
import math

import jax
import jax.numpy as jnp
import numpy as np
from jax import lax
from jax.experimental import pallas as pl
from jax.experimental.pallas import tpu as pltpu

F32 = jnp.float32
BF16 = jnp.bfloat16

D_MODEL = 2048
N_META = 16
HEAD_DIM = 128
DIFF_KV_HEADS = 2
DIFF_REP = 4
DIFF_V_DIM = 2 * HEAD_DIM
SB_KV_HEADS = 4
SB_REP = 4
ROPE_THETA = 10000.0
N_GROUPS = 4
EXPERTS_PER_GROUP = 8
N_EXPERTS = N_GROUPS * EXPERTS_PER_GROUP
D_EXPERT = 512
RMS_EPS = 1e-6
NEG_INF = -1e30
LAM_INIT = 0.8 - 0.6 * math.exp(-0.3 * 0)
SCALE = HEAD_DIM ** -0.5
LOG2E = math.log2(math.e)

C_QD, C_KD, C_VD, C_QS, C_KS, C_VS, C_GATE = 0, 2048, 2560, 3072, 5120, 5632, 6144
IN_COLS = 10240
ATTN_COLS = C_GATE

LANES = 128
PROJ_TN = 512
ROW_BLOCK = 768
ATT_BLOCK = 256
PAGES_PER_STEP = 16
VMEM_LIMIT = 56 * 1024 * 1024


def _cparams(sem):
    return pltpu.CompilerParams(dimension_semantics=sem, vmem_limit_bytes=VMEM_LIMIT)


def _dot(a, b):
    return jnp.dot(a, b, preferred_element_type=F32)


def _dot_nt(a, b):
    return lax.dot_general(a, b, (((1,), (1,)), ((), ())), preferred_element_type=F32)


def _rms(x, g):
    return x * lax.rsqrt(jnp.mean(x * x, axis=-1, keepdims=True) + RMS_EPS) * g


def _rope_table_kernel(pos_ref, inv_ref, cos_ref, sin_ref):
    ang = pos_ref[...] * inv_ref[...]
    lane = lax.broadcasted_iota(jnp.int32, ang.shape, 1)
    s = jnp.sin(ang)
    cos_ref[...] = jnp.cos(ang)
    sin_ref[...] = jnp.where(lane < HEAD_DIM // 2, -s, s)


def _rope_tables(pos, inv, tm):
    t = pos.shape[0]
    return pl.pallas_call(
        _rope_table_kernel,
        grid=(t // tm,),
        in_specs=[pl.BlockSpec((tm, 1), lambda i: (i, 0)), pl.BlockSpec((1, LANES), lambda i: (0, 0))],
        out_specs=[pl.BlockSpec((tm, LANES), lambda i: (i, 0))] * 2,
        out_shape=[jax.ShapeDtypeStruct((t, LANES), F32)] * 2,
        compiler_params=_cparams(("arbitrary",)),
        name="rope_tables",
    )(pos, inv)


N_ROPE_BLOCKS = C_VD // PROJ_TN
N_ATTN_BLOCKS = ATTN_COLS // PROJ_TN


def _proj_kernel(x_ref, g_ref, w_ref, cos_ref, sin_ref, z_ref, zb_ref, h_ref):
    j = pl.program_id(1)

    @pl.when(j == 0)
    def _():
        h_ref[...] = _rms(x_ref[...], g_ref[...]).astype(BF16)

    acc = _dot(h_ref[...], w_ref[...])

    @pl.when(j < N_ROPE_BLOCKS)
    def _():
        cos = cos_ref[...]
        sin = sin_ref[...]
        parts = []
        for c in range(PROJ_TN // HEAD_DIM):
            a = acc[:, c * HEAD_DIM:(c + 1) * HEAD_DIM]
            parts.append(a * cos + pltpu.roll(a, HEAD_DIM // 2, 1) * sin)
        r = jnp.concatenate(parts, axis=1)
        z_ref[...] = r
        zb_ref[...] = (r * jnp.where(j < C_KD // PROJ_TN, SCALE * LOG2E, 1.0)).astype(BF16)

    @pl.when((j >= N_ROPE_BLOCKS) & (j < N_ATTN_BLOCKS))
    def _():
        z_ref[...] = acc
        is_qs = (j >= C_QS // PROJ_TN) & (j < C_KS // PROJ_TN)
        zb_ref[...] = (acc * jnp.where(is_qs, SCALE, 1.0)).astype(BF16)

    @pl.when(j >= N_ATTN_BLOCKS)
    def _():
        z_ref[...] = acc


def _project(x, g, w_bf16, cos, sin, tm):
    t = x.shape[0]
    return pl.pallas_call(
        _proj_kernel,
        grid=(t // tm, IN_COLS // PROJ_TN),
        in_specs=[
            pl.BlockSpec((tm, D_MODEL), lambda i, j: (i, 0)),
            pl.BlockSpec((1, D_MODEL), lambda i, j: (0, 0)),
            pl.BlockSpec((D_MODEL, PROJ_TN), lambda i, j: (0, j)),
            pl.BlockSpec((tm, LANES), lambda i, j: (i, 0)),
            pl.BlockSpec((tm, LANES), lambda i, j: (i, 0)),
        ],
        out_specs=[
            pl.BlockSpec((tm, PROJ_TN), lambda i, j: (i, j)),
            pl.BlockSpec((tm, PROJ_TN), lambda i, j: (i, jnp.minimum(j, N_ATTN_BLOCKS - 1))),
        ],
        out_shape=[jax.ShapeDtypeStruct((t, IN_COLS), F32), jax.ShapeDtypeStruct((t, ATTN_COLS), BF16)],
        scratch_shapes=[pltpu.VMEM((tm, D_MODEL), BF16)],
        compiler_params=_cparams(("arbitrary", "arbitrary")),
        name="in_proj",
    )(x, g, w_bf16, cos, sin)


def _lambda(lq1, lk1, lq2, lk2):
    s1 = jnp.sum(lq1[...] * lk1[...], axis=-1, keepdims=True)
    s2 = jnp.sum(lq2[...] * lk2[...], axis=-1, keepdims=True)
    return jnp.exp(s1) - jnp.exp(s2) + LAM_INIT


def _diff_finish(acc1, l1, acc2, l2, lam, subln):
    w = acc1.shape[1]
    o = acc1 / _lanes(l1, w) - lam * (acc2 / _lanes(l2, w))
    o = o * lax.rsqrt(jnp.mean(o * o, axis=-1, keepdims=True) + RMS_EPS)
    return o * subln * (1.0 - LAM_INIT)


def _lanes(x, width):
    return x if width == LANES else jnp.concatenate([x] * (width // LANES), axis=1)


def _softmax_step(s, v, m_ref, l_ref, acc_ref, idx):
    m_old = m_ref[idx]
    m_new = jnp.maximum(m_old, jnp.max(s, axis=-1, keepdims=True))
    alpha = jnp.exp2(m_old - m_new)
    p = jnp.exp2(s - _lanes(m_new, s.shape[1]))
    l_ref[idx] = alpha * l_ref[idx] + jnp.sum(p, axis=-1, keepdims=True)
    acc_ref[idx] = _lanes(alpha, v.shape[1]) * acc_ref[idx] + _dot(p.astype(BF16), v)
    m_ref[idx] = m_new


def _strict_lower(n):
    r = lax.broadcasted_iota(jnp.int32, (n, n), 0)
    c = lax.broadcasted_iota(jnp.int32, (n, n), 1)
    return jnp.where(r > c, 1.0, 0.0).astype(BF16)


def _log_keep(z):
    return -(jnp.maximum(z, 0.0) + jnp.log(1.0 + jnp.exp(-jnp.abs(z))))


def _split_bf16(x):
    hi = x.astype(BF16)
    return hi, (x - hi.astype(F32)).astype(BF16)


SB_CUTOFF = -104.0


def _stick_step(z, v, u, keep, r_ref, acc_ref, idx):
    lk = _log_keep(z)
    lkm = lk if keep is None else jnp.where(keep, lk, 0.0)
    hi, lo = _split_bf16(lkm)
    suffix = _dot(hi, u) + _dot(lo, u)
    r = r_ref[idx]
    a = jnp.exp(z + lk + suffix + _lanes(r, z.shape[1]))
    if keep is not None:
        a = jnp.where(keep, a, 0.0)
    acc_ref[idx] = acc_ref[idx] + _dot(a.astype(BF16), v)
    r_ref[idx] = r + jnp.sum(lkm, axis=-1, keepdims=True)


def _diff_prompt_kernel(lq1, lk1, lq2, lk2, sg_ref, q_ref, k_ref, v_ref, o_ref, m_ref, l_ref, acc_ref):
    qi = pl.program_id(1)
    tb = ATT_BLOCK
    m_ref[...] = jnp.full(m_ref.shape, NEG_INF, F32)
    l_ref[...] = jnp.zeros(l_ref.shape, F32)
    acc_ref[...] = jnp.zeros(acc_ref.shape, F32)

    def block(kb, causal):
        start = pl.multiple_of(kb * tb, tb)
        kblk = k_ref[pl.ds(start, tb), :]
        vblk = v_ref[pl.ds(start, tb), :]
        for r in range(DIFF_REP):
            for m in range(2):
                c0 = (r * 2 + m) * HEAD_DIM
                s = _dot_nt(q_ref[:, c0:c0 + HEAD_DIM], kblk[:, m * HEAD_DIM:(m + 1) * HEAD_DIM])
                if causal is not None:
                    s = jnp.where(causal, s, NEG_INF)
                _softmax_step(s, vblk, m_ref, l_ref, acc_ref, r * 2 + m)

    def body(kb, carry):
        block(kb, None)
        return carry

    lax.fori_loop(0, qi, body, 0)
    row = lax.broadcasted_iota(jnp.int32, (tb, tb), 0)
    col = lax.broadcasted_iota(jnp.int32, (tb, tb), 1)
    block(qi, col <= row)

    lam = _lambda(lq1, lk1, lq2, lk2)
    for r in range(DIFF_REP):
        o = _diff_finish(acc_ref[2 * r], l_ref[2 * r], acc_ref[2 * r + 1], l_ref[2 * r + 1], lam, sg_ref[...])
        o_ref[:, r * DIFF_V_DIM:(r + 1) * DIFF_V_DIM] = o.astype(BF16)


def _lam_specs(nd):
    zero = (lambda *a: (0, 0))
    return [pl.BlockSpec((1, HEAD_DIM), zero)] * 4 + [pl.BlockSpec((1, DIFF_V_DIM), zero)]


def _diff_prompt(zb, lams, subln):
    t = zb.shape[0]
    tb = ATT_BLOCK
    qw = DIFF_REP * 2 * HEAD_DIM
    return pl.pallas_call(
        _diff_prompt_kernel,
        grid=(DIFF_KV_HEADS, t // tb),
        in_specs=_lam_specs(2) + [
            pl.BlockSpec((tb, qw), lambda g, i: (i, g)),
            pl.BlockSpec((t, 2 * HEAD_DIM), lambda g, i: (0, C_KD // (2 * HEAD_DIM) + g)),
            pl.BlockSpec((t, DIFF_V_DIM), lambda g, i: (0, C_VD // DIFF_V_DIM + g)),
        ],
        out_specs=pl.BlockSpec((tb, DIFF_REP * DIFF_V_DIM), lambda g, i: (i, g)),
        out_shape=jax.ShapeDtypeStruct((t, D_MODEL), BF16),
        scratch_shapes=[
            pltpu.VMEM((2 * DIFF_REP, tb, LANES), F32),
            pltpu.VMEM((2 * DIFF_REP, tb, LANES), F32),
            pltpu.VMEM((2 * DIFF_REP, tb, DIFF_V_DIM), F32),
        ],
        compiler_params=_cparams(("arbitrary", "arbitrary")),
        name="diff_prompt",
    )(*lams, subln, zb, zb, zb)


def _sb_prompt_kernel(q_ref, k_ref, v_ref, o_ref, r_ref, acc_ref):
    qi = pl.program_id(1)
    tb = ATT_BLOCK
    r_ref[...] = jnp.zeros(r_ref.shape, F32)
    acc_ref[...] = jnp.zeros(acc_ref.shape, F32)
    u = _strict_lower(tb)

    def block(kb, keep):
        start = pl.multiple_of(kb * tb, tb)
        kblk = k_ref[pl.ds(start, tb), :]
        vblk = v_ref[pl.ds(start, tb), :]
        for r in range(SB_REP):
            z = _dot_nt(q_ref[:, r * HEAD_DIM:(r + 1) * HEAD_DIM], kblk)
            _stick_step(z, vblk, u, keep, r_ref, acc_ref, r)

    row = lax.broadcasted_iota(jnp.int32, (tb, tb), 0)
    col = lax.broadcasted_iota(jnp.int32, (tb, tb), 1)
    block(qi, col < row)

    def live(c):
        return (c[0] >= 0) & (c[1] > SB_CUTOFF)

    def body(c):
        block(c[0], None)
        return c[0] - 1, jnp.max(r_ref[...])

    lax.while_loop(live, body, (qi - 1, jnp.max(r_ref[...])))
    for r in range(SB_REP):
        o_ref[:, r * HEAD_DIM:(r + 1) * HEAD_DIM] = acc_ref[r].astype(BF16)


def _sb_prompt(zb):
    t = zb.shape[0]
    tb = ATT_BLOCK
    qw = SB_REP * HEAD_DIM
    return pl.pallas_call(
        _sb_prompt_kernel,
        grid=(SB_KV_HEADS, t // tb),
        in_specs=[
            pl.BlockSpec((tb, qw), lambda h, i: (i, C_QS // qw + h)),
            pl.BlockSpec((t, HEAD_DIM), lambda h, i: (0, C_KS // HEAD_DIM + h)),
            pl.BlockSpec((t, HEAD_DIM), lambda h, i: (0, C_VS // HEAD_DIM + h)),
        ],
        out_specs=pl.BlockSpec((tb, qw), lambda h, i: (i, h)),
        out_shape=jax.ShapeDtypeStruct((t, D_MODEL), BF16),
        scratch_shapes=[pltpu.VMEM((SB_REP, tb, LANES), F32), pltpu.VMEM((SB_REP, tb, HEAD_DIM), F32)],
        compiler_params=_cparams(("arbitrary", "arbitrary")),
        name="sb_prompt",
    )(zb, zb, zb)


PAGE = 128
CHUNKS = 4
Q_ROWS = 16
DEC_SEQ = 4


def _page_chunk(ref, c):
    return ref[pl.ds(c, PAGE, stride=CHUNKS), :].astype(BF16)


def _decode_diff(qd_ref, kds, vds, causal, m_ref, l_ref, acc_ref):
    n = len(kds)
    for g in range(DIFF_KV_HEADS):
        v = [[_page_chunk(vd, c * 2 + g) for c in range(2)] for vd in vds]
        alphas, pbs = [], []
        for m in range(2):
            idx = g * 2 + m
            q = qd_ref[idx]
            s = [_dot_nt(q, _page_chunk(kd, idx)) for kd in kds]
            if causal is not None:
                s = [jnp.where(causal, x, NEG_INF) for x in s]
            m_old = m_ref[idx]
            m_new = jnp.maximum(m_old, jnp.max(functools.reduce(jnp.maximum, s), axis=-1, keepdims=True))
            alpha = jnp.exp2(m_old - m_new)
            p = [jnp.exp2(x - m_new) for x in s]
            l_ref[idx] = alpha * l_ref[idx] + jnp.sum(functools.reduce(jnp.add, p), axis=-1, keepdims=True)
            m_ref[idx] = m_new
            alphas.append(alpha)
            pbs.append([x.astype(BF16) for x in p])
        pb = [jnp.concatenate([pbs[0][i], pbs[1][i]], axis=0) for i in range(n)]
        pv = [functools.reduce(jnp.add, [_dot(pb[i], v[i][c]) for i in range(n)]) for c in range(2)]
        pv = jnp.concatenate(pv, axis=1)
        for m in range(2):
            idx = g * 2 + m
            acc_ref[idx] = _lanes(alphas[m], DIFF_V_DIM) * acc_ref[idx] + pv[m * Q_ROWS:(m + 1) * Q_ROWS]


def _decode_stick(qs_ref, kss, vss, strict, u, r_ref, acc_ref):
    n = len(kss)
    rows = n * Q_ROWS
    for h in range(SB_KV_HEADS):
        q = qs_ref[h]
        z = jnp.concatenate([_dot_nt(q, _page_chunk(ks, h)) for ks in kss], axis=0)
        lk = _log_keep(z)
        keep = None if strict is None else jnp.concatenate([strict] * n, axis=0)
        lkm = lk if keep is None else jnp.where(keep, lk, 0.0)
        hi, lo = _split_bf16(lkm)
        s2 = _dot(jnp.concatenate([hi, lo], axis=0), u)
        suffix = s2[:rows] + s2[rows:]
        tot = jnp.sum(lkm, axis=-1, keepdims=True)
        r = r_ref[h]
        offs = []
        for i in range(n):
            offs.append(r)
            r = r + tot[i * Q_ROWS:(i + 1) * Q_ROWS]
        a = jnp.exp(z + lk + suffix + jnp.concatenate(offs, axis=0))
        if keep is not None:
            a = jnp.where(keep, a, 0.0)
        ab = a.astype(BF16)
        pv = [_dot(ab[i * Q_ROWS:(i + 1) * Q_ROWS], _page_chunk(vss[i], h)) for i in range(n)]
        acc_ref[h] = acc_ref[h] + functools.reduce(jnp.add, pv)
        r_ref[h] = r


def _decode_kernel(pt_ref, lq1, lk1, lq2, lk2, sg_ref, qd_ref, qs_ref, nkd, nvd, nks, nvs, *rest):
    n_in = 2 * PAGES_PER_STEP
    pages = rest[:n_in]
    ks_hbm, vs_hbm, od_ref, os_ref = rest[n_in:n_in + 4]
    m_ref, l_ref, accd_ref, r_ref, accs_ref, ks_buf, vs_buf, sb_sem = rest[n_in + 4:]
    b_id = pl.program_id(0)
    s_id = pl.program_id(1)
    u = _strict_lower(PAGE)

    @pl.when(s_id == 0)
    def _():
        m_ref[...] = jnp.full(m_ref.shape, NEG_INF, F32)
        l_ref[...] = jnp.zeros(l_ref.shape, F32)
        accd_ref[...] = jnp.zeros(accd_ref.shape, F32)
        r_ref[...] = jnp.zeros(r_ref.shape, F32)
        accs_ref[...] = jnp.zeros(accs_ref.shape, F32)
        tq = lax.broadcasted_iota(jnp.int32, (Q_ROWS, PAGE), 0) & (DEC_SEQ - 1)
        key = lax.broadcasted_iota(jnp.int32, (Q_ROWS, PAGE), 1)
        _decode_diff(qd_ref, [nkd], [nvd], key <= tq, m_ref, l_ref, accd_ref)
        _decode_stick(qs_ref, [nks], [nvs], key < tq, u, r_ref, accs_ref)

    _decode_diff(qd_ref, pages[0::2], pages[1::2], None, m_ref, l_ref, accd_ref)

    @pl.when(jnp.max(r_ref[...]) > SB_CUTOFF)
    def _():
        n_pages = pl.num_programs(1) * PAGES_PER_STEP

        def copies(p):
            page = pt_ref[b_id, n_pages - 1 - (s_id * PAGES_PER_STEP + p)]
            return (pltpu.make_async_copy(ks_hbm.at[page], ks_buf.at[p], sb_sem),
                    pltpu.make_async_copy(vs_hbm.at[page], vs_buf.at[p], sb_sem))

        for p in range(PAGES_PER_STEP):
            for c in copies(p):
                c.start()
        for p in range(PAGES_PER_STEP):
            for c in copies(p):
                c.wait()
        bufs = range(PAGES_PER_STEP)
        _decode_stick(qs_ref, [ks_buf.at[p] for p in bufs], [vs_buf.at[p] for p in bufs], None, u, r_ref, accs_ref)

    @pl.when(s_id == pl.num_programs(1) - 1)
    def _():
        lam = _lambda(lq1, lk1, lq2, lk2)
        for g in range(DIFF_KV_HEADS):
            o = _diff_finish(accd_ref[2 * g], l_ref[2 * g], accd_ref[2 * g + 1], l_ref[2 * g + 1], lam, sg_ref[...])
            od_ref[g] = o.astype(BF16)
        for h in range(SB_KV_HEADS):
            os_ref[h] = accs_ref[h].astype(BF16)


def _decode(page_table, lams, subln, qd, qs, new_pages, caches):
    nb, n_pages = page_table.shape
    steps = n_pages // PAGES_PER_STEP
    zero2 = lambda b, s, pt: (0, 0)
    per_b4 = lambda b, s, pt: (b, 0, 0, 0)
    per_b3 = lambda b, s, pt: (b, 0, 0)

    def page_spec(p):
        return pl.BlockSpec((None, PAGE * CHUNKS, LANES),
                            lambda b, s, pt: (pt[b, n_pages - 1 - (s * PAGES_PER_STEP + p)], 0, 0))

    in_specs = ([pl.BlockSpec((1, HEAD_DIM), zero2)] * 4 + [pl.BlockSpec((1, DIFF_V_DIM), zero2)]
                + [pl.BlockSpec((None, 4, Q_ROWS, HEAD_DIM), per_b4)] * 2
                + [pl.BlockSpec((None, PAGE * CHUNKS, LANES), per_b3)] * 4)
    page_args = []
    for p in range(PAGES_PER_STEP):
        in_specs += [page_spec(p)] * 2
        page_args += list(caches[:2])
    in_specs += [pl.BlockSpec(memory_space=pl.ANY)] * 2
    page_args += list(caches[2:])
    grid_spec = pltpu.PrefetchScalarGridSpec(
        num_scalar_prefetch=1,
        grid=(nb, steps),
        in_specs=in_specs,
        out_specs=[pl.BlockSpec((None, DIFF_KV_HEADS, Q_ROWS, DIFF_V_DIM), per_b4),
                   pl.BlockSpec((None, SB_KV_HEADS, Q_ROWS, HEAD_DIM), per_b4)],
        scratch_shapes=[
            pltpu.VMEM((4, Q_ROWS, LANES), F32), pltpu.VMEM((4, Q_ROWS, LANES), F32),
            pltpu.VMEM((4, Q_ROWS, DIFF_V_DIM), F32),
            pltpu.VMEM((SB_KV_HEADS, Q_ROWS, LANES), F32), pltpu.VMEM((SB_KV_HEADS, Q_ROWS, HEAD_DIM), F32),
            pltpu.VMEM((PAGES_PER_STEP, PAGE * CHUNKS, LANES), F32),
            pltpu.VMEM((PAGES_PER_STEP, PAGE * CHUNKS, LANES), F32),
            pltpu.SemaphoreType.DMA(()),
        ],
    )
    return pl.pallas_call(
        _decode_kernel,
        grid_spec=grid_spec,
        out_shape=[jax.ShapeDtypeStruct((nb, DIFF_KV_HEADS, Q_ROWS, DIFF_V_DIM), BF16),
                   jax.ShapeDtypeStruct((nb, SB_KV_HEADS, Q_ROWS, HEAD_DIM), BF16)],
        compiler_params=_cparams(("arbitrary", "arbitrary")),
        name="decode_attn",
    )(page_table, *lams, subln, qd, qs, *new_pages, *page_args)


def _merge_kernel(od_ref, os_ref, gd_ref, gs_ref, wd_ref, ws_ref, o_ref):
    g_d = 1.0 / (1.0 + jnp.exp(-gd_ref[...]))
    g_s = 1.0 / (1.0 + jnp.exp(-gs_ref[...]))
    o_ref[...] = (g_d * _dot(od_ref[...], wd_ref[...]) + g_s * _dot(os_ref[...], ws_ref[...])).astype(BF16)


def _merge(od, os_, z32, wbd, wbs, tm):
    t = od.shape[0]
    tn = PROJ_TN
    gd0 = C_GATE // tn
    gs0 = (C_GATE + D_MODEL) // tn
    return pl.pallas_call(
        _merge_kernel,
        grid=(t // tm, D_MODEL // tn),
        in_specs=[
            pl.BlockSpec((tm, D_MODEL), lambda i, j: (i, 0)),
            pl.BlockSpec((tm, D_MODEL), lambda i, j: (i, 0)),
            pl.BlockSpec((tm, tn), lambda i, j: (i, gd0 + j)),
            pl.BlockSpec((tm, tn), lambda i, j: (i, gs0 + j)),
            pl.BlockSpec((D_MODEL, tn), lambda i, j: (0, j)),
            pl.BlockSpec((D_MODEL, tn), lambda i, j: (0, j)),
        ],
        out_specs=pl.BlockSpec((tm, tn), lambda i, j: (i, j)),
        out_shape=jax.ShapeDtypeStruct((t, D_MODEL), BF16),
        compiler_params=_cparams(("arbitrary", "arbitrary")),
        name="merge",
    )(od, os_, z32, z32, wbd, wbs)


def _out_kernel(x_ref, m_ref, w_ref, o_ref):
    o_ref[...] = x_ref[...] + _dot(m_ref[...], w_ref[...])


def _out_proj(x, merged, w_out, tm):
    t = x.shape[0]
    tn = PROJ_TN
    return pl.pallas_call(
        _out_kernel,
        grid=(t // tm, D_MODEL // tn),
        in_specs=[
            pl.BlockSpec((tm, tn), lambda i, j: (i, j)),
            pl.BlockSpec((tm, D_MODEL), lambda i, j: (i, 0)),
            pl.BlockSpec((D_MODEL, tn), lambda i, j: (0, j)),
        ],
        out_specs=pl.BlockSpec((tm, tn), lambda i, j: (i, j)),
        out_shape=jax.ShapeDtypeStruct((t, D_MODEL), F32),
        compiler_params=_cparams(("arbitrary", "arbitrary")),
        name="out_proj",
    )(x, merged, w_out)


ROW_CHUNKS = D_MODEL // LANES


def _store_row_major(ref, x):
    rows = x.shape[0]
    for c in range(ROW_CHUNKS):
        ref[pl.ds(c, rows, stride=ROW_CHUNKS), :] = x[:, c * LANES:(c + 1) * LANES]


def _load_row_major(ref, rows):
    return jnp.concatenate([ref[pl.ds(c, rows, stride=ROW_CHUNKS), :] for c in range(ROW_CHUNKS)], axis=1)


def _router_kernel(x_ref, g_ref, w_ref, b_ref, hn_ref, comb_ref, route_ref, counts_ref, carry_ref):
    hn32 = _rms(x_ref[...], g_ref[...])
    hn = hn32.astype(BF16)
    if hn_ref.dtype == BF16:
        hn_ref[...] = hn
    else:
        _store_row_major(hn_ref, hn32)
    logits = _dot(hn, w_ref[...]) + b_ref[...]
    gl = logits[:, :LANES]
    el = logits[:, LANES:]
    lane = lax.broadcasted_iota(jnp.int32, gl.shape, 1).astype(F32)
    far = float(LANES)

    def first_max(v):
        mx = jnp.max(v, axis=-1, keepdims=True)
        return mx, jnp.min(jnp.where(v == mx, lane, far), axis=-1, keepdims=True)

    gl = jnp.where(lane < N_GROUPS, gl, NEG_INF)
    gmax, grp = first_max(gl)
    p_grp = 1.0 / jnp.sum(jnp.exp(gl - gmax), axis=-1, keepdims=True)
    lo = grp * EXPERTS_PER_GROUP
    es = jnp.where((lane >= lo) & (lane < lo + EXPERTS_PER_GROUP), el, NEG_INF)
    v1, i1 = first_max(es)
    es2 = jnp.where(lane == i1, NEG_INF, es)
    v2, i2 = first_max(es2)
    e2 = jnp.exp(v2 - v1)
    w1 = 1.0 / (1.0 + e2)
    w2 = e2 / (1.0 + e2)
    ga = p_grp * w1
    gb = p_grp * w2
    oh_a = jnp.where(lane == i1, 1.0, 0.0)
    oh_b = jnp.where(lane == i2, 1.0, 0.0)
    comb_ref[...] = ga * oh_a + gb * oh_b

    @pl.when(pl.program_id(0) == 0)
    def _():
        carry_ref[...] = jnp.zeros(carry_ref.shape, F32)

    tm = x_ref.shape[0]
    oh = oh_a + oh_b
    before = _dot(_strict_lower(tm), oh.astype(BF16)) + carry_ref[...]
    rank_a = jnp.sum(before * oh_a, axis=-1, keepdims=True)
    rank_b = jnp.sum(before * oh_b, axis=-1, keepdims=True)
    carry_ref[...] = carry_ref[...] + jnp.sum(oh, axis=0, keepdims=True)
    counts_ref[...] = carry_ref[...]
    fields = (i1, i2, ga, gb, rank_a, rank_b)
    route_ref[...] = functools.reduce(jnp.add, [jnp.where(lane == float(k), f, 0.0) for k, f in enumerate(fields)])


def _router(x2, g, w_r, b_r, tm, routed):
    t = x2.shape[0]
    if routed:
        hn_spec = pl.BlockSpec((tm * ROW_CHUNKS, LANES), lambda i: (i, 0))
        hn_shape = jax.ShapeDtypeStruct((t * ROW_CHUNKS, LANES), F32)
    else:
        hn_spec = pl.BlockSpec((tm, D_MODEL), lambda i: (i, 0))
        hn_shape = jax.ShapeDtypeStruct((t, D_MODEL), BF16)
    return pl.pallas_call(
        _router_kernel,
        grid=(t // tm,),
        in_specs=[
            pl.BlockSpec((tm, D_MODEL), lambda i: (i, 0)),
            pl.BlockSpec((1, D_MODEL), lambda i: (0, 0)),
            pl.BlockSpec((D_MODEL, 2 * LANES), lambda i: (0, 0)),
            pl.BlockSpec((1, 2 * LANES), lambda i: (0, 0)),
        ],
        out_specs=[hn_spec, pl.BlockSpec((tm, LANES), lambda i: (i, 0)),
                   pl.BlockSpec((tm, LANES), lambda i: (i, 0)), pl.BlockSpec((1, LANES), lambda i: (0, 0))],
        out_shape=[hn_shape, jax.ShapeDtypeStruct((t, LANES), F32),
                   jax.ShapeDtypeStruct((t, LANES), F32), jax.ShapeDtypeStruct((1, LANES), F32)],
        scratch_shapes=[pltpu.VMEM((1, LANES), F32)],
        compiler_params=_cparams(("arbitrary",)),
        name="router",
    )(x2, g, w_r, b_r)


MOE_TM = 256


DMA_UNROLL = 8


def _moe_routed_kernel(te_ref, dprev_ref, dcur_ref, dnext_ref, wg_ref, wu_ref, wd_ref, hn_hbm, y_hbm,
                       xbuf, ybuf, gsem, ssem, *, n):
    j = pl.program_id(0)
    last = pl.num_programs(0) - 1
    cur = j % 2
    next_used = (j < last) & (te_ref[jnp.minimum(j + 1, last)] < N_EXPERTS)

    def rows_at(r):
        return pl.ds(pl.multiple_of(r * ROW_CHUNKS, ROW_CHUNKS), ROW_CHUNKS)

    def row_in(dref, b, r):
        d = dref[0, r]
        src = jnp.where(d < 0, 0, jnp.where(d >= n, d - n, d))
        return pltpu.make_async_copy(hn_hbm.at[rows_at(src), :], xbuf.at[b, rows_at(r), :], gsem.at[b])

    def row_out(dref, b, r):
        return pltpu.make_async_copy(ybuf.at[b, rows_at(r), :], y_hbm.at[rows_at(dref[0, r]), :], ssem.at[b])

    def each_row(fn):
        def body(r, c):
            fn(r)
            return c
        lax.fori_loop(0, MOE_TM, body, 0, unroll=DMA_UNROLL)

    def each_real_row(dref, fn):
        each_row(lambda r: pl.when(dref[0, r] >= 0)(lambda: fn(r)))

    def each_row_pair(fn):
        def body(i, c):
            fn(2 * i, 0)
            fn(2 * i + 1, 1)
            return c
        lax.fori_loop(0, MOE_TM // 2, body, 0, unroll=DMA_UNROLL // 2)

    @pl.when(te_ref[j] < N_EXPERTS)
    def _():
        @pl.when(j == 0)
        def _():
            each_row_pair(lambda r, pr: row_in(dcur_ref, cur, r).start(priority=pr))

        @pl.when(next_used)
        def _():
            each_row_pair(lambda r, pr: row_in(dnext_ref, 1 - cur, r).start(priority=pr))

        each_row(lambda r: row_in(dcur_ref, cur, r).wait())
        x = _load_row_major(xbuf.at[cur], MOE_TM).astype(BF16)
        gate = _dot(x, wg_ref[...].astype(BF16))
        hid = gate * (1.0 / (1.0 + jnp.exp(-gate))) * _dot(x, wu_ref[...].astype(BF16))
        _store_row_major(ybuf.at[cur], _dot(hid.astype(BF16), wd_ref[...].astype(BF16)))

        @pl.when(j > 0)
        def _():
            each_real_row(dprev_ref, lambda r: row_out(dprev_ref, 1 - cur, r).wait())

        each_row_pair(lambda r, pr: pl.when(dcur_ref[0, r] >= 0)(
            lambda: row_out(dcur_ref, cur, r).start(priority=pr)))

        @pl.when(jnp.logical_not(next_used))
        def _():
            each_real_row(dcur_ref, lambda r: row_out(dcur_ref, cur, r).wait())


def _moe_routed(hn, route, counts, wg, wu, wd):
    n = hn.shape[0] // ROW_CHUNKS
    n_tiles = (2 * n) // MOE_TM + N_EXPERTS
    p = n_tiles * MOE_TM
    experts = route[:, 0:2].astype(jnp.int32)
    ranks = route[:, 4:6].astype(jnp.int32)
    cnt = counts[0, :N_EXPERTS].astype(jnp.int32)
    padded = (cnt + MOE_TM - 1) // MOE_TM * MOE_TM
    ends = jnp.cumsum(padded)
    slot = ((ends - padded)[experts] + ranks).reshape(-1)
    tok = jnp.arange(n, dtype=jnp.int32)
    dst = jnp.full((p,), -1, jnp.int32).at[slot].set(jnp.stack([tok, tok + n], axis=1).reshape(-1))
    tile_expert = jnp.sum(jnp.arange(n_tiles, dtype=jnp.int32)[:, None] * MOE_TM >= ends[None, :], axis=1)
    tile_expert = tile_expert.astype(jnp.int32)

    def w_spec(shape):
        return pl.BlockSpec((None,) + shape, lambda j, te: (jnp.minimum(te[j], N_EXPERTS - 1), 0, 0))

    def dst_spec(shift):
        return pl.BlockSpec((None, 1, MOE_TM), lambda j, te: (jnp.clip(j + shift, 0, n_tiles - 1), 0, 0),
                            memory_space=pltpu.SMEM)

    grid_spec = pltpu.PrefetchScalarGridSpec(
        num_scalar_prefetch=1,
        grid=(n_tiles,),
        in_specs=[dst_spec(-1), dst_spec(0), dst_spec(1),
                  w_spec((D_MODEL, D_EXPERT)), w_spec((D_MODEL, D_EXPERT)), w_spec((D_EXPERT, D_MODEL)),
                  pl.BlockSpec(memory_space=pl.ANY)],
        out_specs=pl.BlockSpec(memory_space=pl.ANY),
        scratch_shapes=[pltpu.VMEM((2, MOE_TM * ROW_CHUNKS, LANES), F32),
                        pltpu.VMEM((2, MOE_TM * ROW_CHUNKS, LANES), F32),
                        pltpu.SemaphoreType.DMA((2,)), pltpu.SemaphoreType.DMA((2,))],
    )
    dst = dst.reshape(n_tiles, 1, MOE_TM)
    return pl.pallas_call(
        functools.partial(_moe_routed_kernel, n=n),
        grid_spec=grid_spec,
        out_shape=jax.ShapeDtypeStruct((2 * n * ROW_CHUNKS, LANES), F32),
        compiler_params=_cparams(("arbitrary",)),
        name="moe_routed",
    )(tile_expert, dst, dst, dst, wg, wu, wd, hn)


def _combine_kernel(x_ref, route_ref, ya_ref, yb_ref, g_ref, o_ref):
    tm = x_ref.shape[0]
    route = route_ref[...]
    lane = lax.broadcasted_iota(jnp.int32, route.shape, 1)
    ga = jnp.sum(jnp.where(lane == 2, route, 0.0), axis=-1, keepdims=True)
    gb = jnp.sum(jnp.where(lane == 3, route, 0.0), axis=-1, keepdims=True)
    moe = ga * _load_row_major(ya_ref, tm) + gb * _load_row_major(yb_ref, tm)
    o_ref[...] = _rms(x_ref[...] + moe, g_ref[...])


def _combine(x2, route, y2, gf, tm):
    t = x2.shape[0]
    nblk = t // tm
    return pl.pallas_call(
        _combine_kernel,
        grid=(nblk,),
        in_specs=[pl.BlockSpec((tm, D_MODEL), lambda i: (i, 0)),
                  pl.BlockSpec((tm, LANES), lambda i: (i, 0)),
                  pl.BlockSpec((tm * ROW_CHUNKS, LANES), lambda i: (i, 0)),
                  pl.BlockSpec((tm * ROW_CHUNKS, LANES), lambda i: (i + nblk, 0)),
                  pl.BlockSpec((1, D_MODEL), lambda i: (0, 0))],
        out_specs=pl.BlockSpec((tm, D_MODEL), lambda i: (i, 0)),
        out_shape=jax.ShapeDtypeStruct((t, D_MODEL), F32),
        compiler_params=_cparams(("arbitrary",)),
        name="moe_combine",
    )(x2, route, y2, y2, gf)


def _moe_kernel(hn_ref, comb_ref, x_ref, wg_ref, wu_ref, wd_ref, gf_ref, y_ref):
    e = pl.program_id(1)

    @pl.when(e == 0)
    def _():
        y_ref[...] = x_ref[...]

    hn = hn_ref[...]
    lane = lax.broadcasted_iota(jnp.int32, comb_ref.shape, 1)
    c = jnp.sum(jnp.where(lane == e, comb_ref[...], 0.0), axis=-1, keepdims=True)
    gate = _dot(hn, wg_ref[...].astype(BF16))
    hid = gate * (1.0 / (1.0 + jnp.exp(-gate))) * _dot(hn, wu_ref[...].astype(BF16))
    y_ref[...] += _dot((hid * c).astype(BF16), wd_ref[...].astype(BF16))

    @pl.when(e == pl.num_programs(1) - 1)
    def _():
        y_ref[...] = _rms(y_ref[...], gf_ref[...])


def _moe(hn, comb, x2, wg, wu, wd, gf, tm):
    t = hn.shape[0]
    return pl.pallas_call(
        _moe_kernel,
        grid=(t // tm, N_EXPERTS),
        in_specs=[
            pl.BlockSpec((tm, D_MODEL), lambda i, e: (i, 0)),
            pl.BlockSpec((tm, LANES), lambda i, e: (i, 0)),
            pl.BlockSpec((tm, D_MODEL), lambda i, e: (i, 0)),
            pl.BlockSpec((None, D_MODEL, D_EXPERT), lambda i, e: (e, 0, 0)),
            pl.BlockSpec((None, D_MODEL, D_EXPERT), lambda i, e: (e, 0, 0)),
            pl.BlockSpec((None, D_EXPERT, D_MODEL), lambda i, e: (e, 0, 0)),
            pl.BlockSpec((1, D_MODEL), lambda i, e: (0, 0)),
        ],
        out_specs=pl.BlockSpec((tm, D_MODEL), lambda i, e: (i, 0)),
        out_shape=jax.ShapeDtypeStruct((t, D_MODEL), F32),
        compiler_params=_cparams(("arbitrary", "arbitrary")),
        name="moe",
    )(hn, comb, x2, wg, wu, wd, gf)


def _post_attention(x, od, os_, z32, w, tm):
    merged = _merge(od, os_, z32, w["wbd"], w["wbs"], tm)
    x2 = _out_proj(x, merged, w["wout"], tm)
    if x.shape[0] // MOE_TM < N_EXPERTS:
        hn, comb, _, _ = _router(x2, w["g_ffn"], w["w_r"], w["b_r"], tm, False)
        return _moe(hn, comb, x2, w["wg"], w["wu"], w["wd"], w["g_final"], tm)
    hn, _, route, counts = _router(x2, w["g_ffn"], w["w_r"], w["b_r"], tm, True)
    y2 = _moe_routed(hn, route, counts, w["wg"], w["wu"], w["wd"])
    return _combine(x2, route, y2, w["g_final"], tm)


def _pad_rows(a, rows):
    return jnp.pad(a, ((0, rows - a.shape[0]),) + ((0, 0),) * (a.ndim - 1))


def kernel(x_prompt, x_sample, cache_k_diff, cache_v_diff, cache_k_sb, cache_v_sb, page_table, meta_tokens, norm_mix_g, w_in, lambda_q1, lambda_k1, lambda_q2, lambda_k2, subln_g, w_branch_diff, w_branch_sb, w_out, norm_ffn_g, w_router_group, b_router_group, w_router_expert, b_router_expert, w_expert_gate, w_expert_up, w_expert_down, norm_final_g):
    assert x_prompt.shape[0] == 1 and norm_mix_g.shape[0] == 1
    seq = x_prompt.shape[1]
    tp_real = N_META + seq
    tp = -(-tp_real // ROW_BLOCK) * ROW_BLOCK
    assert tp % ATT_BLOCK == 0
    nb, dec_seq, _ = x_sample.shape
    ts = nb * dec_seq
    n_phys = cache_k_diff.shape[1]
    past_len = page_table.shape[1] * PAGE
    assert dec_seq == DEC_SEQ and cache_k_diff.shape[2] == PAGE

    w_in_b = w_in[0].astype(BF16)
    w_r = jnp.zeros((D_MODEL, 2 * LANES), F32)
    w_r = w_r.at[:, :N_GROUPS].set(w_router_group[0]).at[:, LANES:LANES + N_EXPERTS].set(w_router_expert[0])
    b_r = jnp.zeros((1, 2 * LANES), F32)
    b_r = b_r.at[0, :N_GROUPS].set(b_router_group[0]).at[0, LANES:LANES + N_EXPERTS].set(b_router_expert[0])
    w = dict(
        wbd=w_branch_diff[0].astype(BF16), wbs=w_branch_sb[0].astype(BF16), wout=w_out[0].astype(BF16),
        g_ffn=norm_ffn_g, w_r=w_r.astype(BF16), b_r=b_r,
        wg=w_expert_gate[0], wu=w_expert_up[0], wd=w_expert_down[0],
        g_final=norm_final_g.reshape(1, D_MODEL),
    )
    lams = (lambda_q1, lambda_k1, lambda_q2, lambda_k2)

    half = HEAD_DIM // 2
    inv = ROPE_THETA ** (-jnp.arange(half, dtype=F32) * 2.0 / HEAD_DIM)
    inv = jnp.concatenate([inv, inv]).reshape(1, LANES)

    xp = _pad_rows(jnp.concatenate([meta_tokens.astype(F32), x_prompt[0]], axis=0), tp)
    pos_p = jnp.arange(tp, dtype=jnp.int32).astype(F32).reshape(tp, 1)
    cos_p, sin_p = _rope_tables(pos_p, inv, ROW_BLOCK)
    z32_p, zb_p = _project(xp, norm_mix_g, w_in_b, cos_p, sin_p, ROW_BLOCK)
    od_p = _diff_prompt(zb_p, lams, subln_g)
    os_p = _sb_prompt(zb_p)
    y_p = _post_attention(xp, od_p, os_p, z32_p, w, ROW_BLOCK)

    xs = x_sample.reshape(ts, D_MODEL)
    pos_s = jnp.tile(past_len + jnp.arange(dec_seq, dtype=jnp.int32), nb).astype(F32).reshape(ts, 1)
    cos_s, sin_s = _rope_tables(pos_s, inv, ts)
    z32_s, zb_s = _project(xs, norm_mix_g, w_in_b, cos_s, sin_s, ts)

    qd = zb_s[:, C_QD:C_KD].reshape(nb, dec_seq, DIFF_KV_HEADS, DIFF_REP, 2, HEAD_DIM)
    qd = qd.transpose(0, 2, 4, 3, 1, 5).reshape(nb, 4, Q_ROWS, HEAD_DIM)
    qs = zb_s[:, C_QS:C_KS].reshape(nb, dec_seq, SB_KV_HEADS, SB_REP, HEAD_DIM)
    qs = qs.transpose(0, 2, 3, 1, 4).reshape(nb, SB_KV_HEADS, Q_ROWS, HEAD_DIM)

    def new_page(a):
        return jnp.pad(a.reshape(nb, dec_seq * CHUNKS, LANES), ((0, 0), (0, (PAGE - dec_seq) * CHUNKS), (0, 0)))

    nvd = z32_s[:, C_VD:C_QS].reshape(nb, dec_seq, DIFF_KV_HEADS, 2, LANES).transpose(0, 1, 3, 2, 4)
    new_pages = (new_page(z32_s[:, C_KD:C_VD]), new_page(nvd),
                 new_page(z32_s[:, C_KS:C_VS]), new_page(z32_s[:, C_VS:C_GATE]))
    cvd = cache_v_diff[0].reshape(n_phys, PAGE, DIFF_KV_HEADS, 2, LANES).transpose(0, 1, 3, 2, 4)
    caches = (cache_k_diff[0].reshape(n_phys, PAGE * CHUNKS, LANES), cvd.reshape(n_phys, PAGE * CHUNKS, LANES),
              cache_k_sb[0].reshape(n_phys, PAGE * CHUNKS, LANES), cache_v_sb[0].reshape(n_phys, PAGE * CHUNKS, LANES))
    od_s, os_s = _decode(page_table, lams, subln_g, qd, qs, new_pages, caches)
    od_s = od_s.reshape(nb, DIFF_KV_HEADS, DIFF_REP, dec_seq, DIFF_V_DIM).transpose(0, 3, 1, 2, 4).reshape(ts, D_MODEL)
    os_s = os_s.reshape(nb, SB_KV_HEADS, SB_REP, dec_seq, HEAD_DIM).transpose(0, 3, 1, 2, 4).reshape(ts, D_MODEL)
    y_s = _post_attention(xs, od_s, os_s, z32_s, w, ts)

    def kv(z, rows, lead):
        return (z[:rows, C_KD:C_VD].reshape(lead + (DIFF_KV_HEADS, 2, HEAD_DIM)),
                z[:rows, C_VD:C_QS].reshape(lead + (DIFF_KV_HEADS, DIFF_V_DIM)),
                z[:rows, C_KS:C_VS].reshape(lead + (SB_KV_HEADS, HEAD_DIM)),
                z[:rows, C_VS:C_GATE].reshape(lead + (SB_KV_HEADS, HEAD_DIM)))

    y_prompt = y_p[N_META:tp_real].reshape(1, seq, D_MODEL)
    y_sample = y_s.reshape(nb, dec_seq, D_MODEL)
    return (y_prompt, y_sample) + kv(z32_p, tp_real, (1, 1, tp_real)) + kv(z32_s, ts, (1, nb, dec_seq))
```
